```python
import math
import jax, jax.numpy as jnp
from jax import lax
import numpy as np

D_MODEL = 2048
BATCH = 4
SEQ = 2048
DEPTH = 1

CHUNK = 64
PLE_DIM = 256
GM_WIDTH = D_MODEL
GM_GROUPS = 8
GM_GROUP_DIM = GM_WIDTH // GM_GROUPS
GM_BLOCK = 128
CV_WIDTH = D_MODEL
CV_KERNEL = 31
PEER_HEADS = 8
PEER_NKEYS = 128
PEER_EXPERTS = PEER_NKEYS * PEER_NKEYS
PEER_QDIM = 256
PEER_HALF = PEER_QDIM // 2
PEER_TOPK = 16
PEER_ACTIVE = PEER_HEADS * PEER_TOPK
PEER_TOKEN_BLOCK = 128
ALPHA = (2.0 * DEPTH) ** 0.25
BETA = (8.0 * DEPTH) ** -0.25
LN_EPS = 1e-5
IN_COLS = 2 * GM_WIDTH + 2 * CV_WIDTH + 2 * D_MODEL
IN_SPLITS = [GM_WIDTH, 2 * GM_WIDTH, 2 * GM_WIDTH + CV_WIDTH,
             2 * GM_WIDTH + 2 * CV_WIDTH, 2 * GM_WIDTH + 2 * CV_WIDTH + D_MODEL]

kernel_name = 'hybrid_gmlp_conformer_peer_block'


def layer_norm(x, g, b):
    xf = x.astype(jnp.float32)
    mu = jnp.mean(xf, axis=-1, keepdims=True)
    var = jnp.mean(jnp.square(xf - mu), axis=-1, keepdims=True)
    return ((xf - mu) * lax.rsqrt(var + LN_EPS) * g + b).astype(x.dtype)


def chunk_causal_mask(dtype):
    pos = jnp.arange(GM_BLOCK)
    return (pos[None, :] // CHUNK <= pos[:, None] // CHUNK).astype(dtype)


def spatial_gating(u, v, ln_g, ln_b, w_s, b_s):
    bsz, seq, _ = v.shape
    nblk = seq // GM_BLOCK
    v = v.reshape(bsz, seq, GM_GROUPS, GM_GROUP_DIM)
    v = layer_norm(v, ln_g.reshape(GM_GROUPS, GM_GROUP_DIM), ln_b.reshape(GM_GROUPS, GM_GROUP_DIM))
    v = v.reshape(bsz, nblk, GM_BLOCK, GM_GROUPS, GM_GROUP_DIM)
    w = w_s * chunk_causal_mask(w_s.dtype)
    mixed = jnp.einsum('gij,bnjgd->bnigd', w, v) + b_s.T[None, None, :, :, None]
    return u * mixed.reshape(bsz, seq, GM_WIDTH)


def causal_depthwise_conv(a, w, b):
    c = a.shape[-1]
    y = lax.conv_general_dilated(a, w[:, None, :].astype(a.dtype), window_strides=(1,),
                                 padding=[(CV_KERNEL - 1, 0)],
                                 dimension_numbers=('NWC', 'WIO', 'NWC'),
                                 feature_group_count=c)
    return y + b


def peer_route(xf, w_q, sub_keys):
    t = xf.shape[0]
    q = (xf @ w_q).reshape(t, PEER_HEADS, 2, PEER_HALF)
    s = jnp.einsum('thcd,hckd->thck', q, sub_keys).astype(jnp.float32)
    top_s, top_i = lax.top_k(s, PEER_TOPK)
    cand = (top_s[:, :, 0, :, None] + top_s[:, :, 1, None, :]).reshape(t, PEER_HEADS, PEER_TOPK * PEER_TOPK)
    best_s, best_c = lax.top_k(cand, PEER_TOPK)
    i1 = jnp.take_along_axis(top_i[:, :, 0], best_c // PEER_TOPK, axis=-1)
    i2 = jnp.take_along_axis(top_i[:, :, 1], best_c % PEER_TOPK, axis=-1)
    idx = i1 * PEER_NKEYS + i2
    gate = jax.nn.softmax(best_s, axis=-1)
    return idx.reshape(t, PEER_ACTIVE), gate.reshape(t, PEER_ACTIVE)


def peer_experts(xf, idx, gate, u_tab, v_tab):
    t, d = xf.shape
    nblk = t // PEER_TOKEN_BLOCK

    def block(args):
        xb, ib, gb = args
        u = jnp.take(u_tab, ib, axis=0)
        h = jnp.einsum('tkd,td->tk', u, xb).astype(jnp.float32)
        act = (gb * jax.nn.gelu(h, approximate=False)).astype(xb.dtype)
        v = jnp.take(v_tab, ib, axis=0)
        return jnp.einsum('tk,tkd->td', act, v)

    y = lax.map(block, (xf.reshape(nblk, PEER_TOKEN_BLOCK, d),
                        idx.reshape(nblk, PEER_TOKEN_BLOCK, PEER_ACTIVE),
                        gate.reshape(nblk, PEER_TOKEN_BLOCK, PEER_ACTIVE)))
    return y.reshape(t, d)


def setup_inputs(seed: int = 0) -> dict:
    key = jax.random.key(seed)
    ks = jax.random.split(key, 32)
    f32 = jnp.float32
    n = lambda k, shape: jax.random.normal(k, shape, f32)
    L = DEPTH
    return {
        'x': n(ks[0], (BATCH, SEQ, D_MODEL)),
        'p': n(ks[1], (DEPTH, BATCH, SEQ, PLE_DIM)),
        'w_in': n(ks[2], (L, D_MODEL, IN_COLS)) * D_MODEL ** -0.5,
        'b_in': n(ks[3], (L, IN_COLS)) * 0.01,
        'gm_ln_g': 1.0 + 0.01 * n(ks[4], (L, GM_WIDTH)),
        'gm_ln_b': 0.01 * n(ks[5], (L, GM_WIDTH)),
        'gm_ws': n(ks[6], (L, GM_GROUPS, GM_BLOCK, GM_BLOCK)) * GM_BLOCK ** -0.5,
        'gm_bs': 1.0 + 0.01 * n(ks[7], (L, GM_GROUPS, GM_BLOCK)),
        'w_gm_out': n(ks[8], (L, GM_WIDTH, D_MODEL)) * GM_WIDTH ** -0.5 * BETA,
        'cv_w': n(ks[9], (L, CV_KERNEL, CV_WIDTH)) * CV_KERNEL ** -0.5,
        'cv_b': 0.01 * n(ks[10], (L, CV_WIDTH)),
        'cv_ln_g': 1.0 + 0.01 * n(ks[11], (L, CV_WIDTH)),
        'cv_ln_b': 0.01 * n(ks[12], (L, CV_WIDTH)),
        'w_cv_out': n(ks[13], (L, CV_WIDTH, D_MODEL)) * CV_WIDTH ** -0.5 * BETA,
        'w_o': n(ks[14], (L, D_MODEL, D_MODEL)) * D_MODEL ** -0.5 * BETA,
        'ln1_g': 1.0 + 0.01 * n(ks[15], (L, D_MODEL)),
        'ln1_b': 0.01 * n(ks[16], (L, D_MODEL)),
        'peer_wq': n(ks[17], (L, D_MODEL, PEER_HEADS * PEER_QDIM)) * D_MODEL ** -0.5,
        'peer_keys': n(ks[18], (L, PEER_HEADS, 2, PEER_NKEYS, PEER_HALF)) * PEER_HALF ** -0.5,
        'peer_u': n(ks[19], (L, PEER_EXPERTS, D_MODEL)) * D_MODEL ** -0.5,
        'peer_v': n(ks[20], (L, PEER_EXPERTS, D_MODEL)) * PEER_ACTIVE ** -0.5 * BETA,
        'ln2_g': 1.0 + 0.01 * n(ks[21], (L, D_MODEL)),
        'ln2_b': 0.01 * n(ks[22], (L, D_MODEL)),
        'ple_w_gate': n(ks[23], (L, D_MODEL, D_MODEL)) * D_MODEL ** -0.5,
        'ple_w_proj': n(ks[24], (L, PLE_DIM, D_MODEL)) * PLE_DIM ** -0.5 * BETA,
    }


def reference(x, p, w_in, b_in, gm_ln_g, gm_ln_b, gm_ws, gm_bs, w_gm_out, cv_w, cv_b,
              cv_ln_g, cv_ln_b, w_cv_out, w_o, ln1_g, ln1_b, peer_wq, peer_keys,
              peer_u, peer_v, ln2_g, ln2_b, ple_w_gate, ple_w_proj):
    bsz, seq, d = x.shape
    for i in range(DEPTH):
        z = x @ w_in[i] + b_in[i]
        zu, zv, za, zb, zga, zgb = jnp.split(z, IN_SPLITS, axis=-1)
        sg = spatial_gating(jax.nn.gelu(zu, approximate=False), jax.nn.gelu(zv, approximate=False),
                            gm_ln_g[i], gm_ln_b[i], gm_ws[i], gm_bs[i])
        y_a = sg @ w_gm_out[i]
        c = causal_depthwise_conv(za * jax.nn.sigmoid(zb), cv_w[i], cv_b[i])
        c = jax.nn.silu(layer_norm(c, cv_ln_g[i], cv_ln_b[i]))
        y_b = c @ w_cv_out[i]
        mix = (jax.nn.sigmoid(zga) * y_a + jax.nn.sigmoid(zgb) * y_b) @ w_o[i]
        x = layer_norm(ALPHA * x + mix, ln1_g[i], ln1_b[i])
        xf = x.reshape(bsz * seq, d)
        idx, gate = peer_route(xf, peer_wq[i], peer_keys[i])
        y_ff = peer_experts(xf, idx, gate, peer_u[i], peer_v[i]).reshape(bsz, seq, d)
        x = layer_norm(ALPHA * x + y_ff, ln2_g[i], ln2_b[i])
        x = x + jax.nn.sigmoid(x @ ple_w_gate[i]) * (p[i] @ ple_w_proj[i])
    return x
```

```python
import functools

import jax
import jax.numpy as jnp
from jax import lax
from jax.experimental import pallas as pl
from jax.experimental.pallas import tpu as pltpu

LN_EPS = 1e-5
CHUNK = 64
PEER_TOPK = 16
SQRT_HALF = 0.7071067811865476
LANES = 128
HALO = 32
VMEM_LIMIT = 56 * 1024 * 1024

F32 = jnp.float32
BF16 = jnp.bfloat16


def _gelu(x):
    return 0.5 * x * (1.0 + lax.erf(x * SQRT_HALF))


def _sigmoid(x):
    return 1.0 / (1.0 + jnp.exp(-x))


def _layer_norm(x, g, b):
    mu = jnp.mean(x, axis=-1, keepdims=True)
    d = x - mu
    var = jnp.mean(d * d, axis=-1, keepdims=True)
    return d * lax.rsqrt(var + LN_EPS) * g + b


def _dot(a, b):
    return jnp.dot(a, b, preferred_element_type=F32)


def _params(*sem):
    return pltpu.CompilerParams(dimension_semantics=sem, vmem_limit_bytes=VMEM_LIMIT)


def _inproj_kernel(x_ref, wu_ref, wv_ref, wa_ref, wb_ref, wga_ref, wgb_ref,
                   bu_ref, bv_ref, ba_ref, bb_ref, bga_ref, bgb_ref,
                   lng_ref, lnb_ref, ws_ref, bs_ref,
                   sg_ref, glu_ref, ga_ref, gb_ref):
    x = x_ref[...]
    tm = x.shape[0]
    blk = ws_ref.shape[1]

    def proj(w_ref, b_ref):
        return _dot(x, w_ref[...]) + b_ref[...]

    u = _gelu(proj(wu_ref, bu_ref))
    v = _gelu(proj(wv_ref, bv_ref))
    vn = _layer_norm(v, lng_ref[...], lnb_ref[...]).astype(BF16)
    row = lax.broadcasted_iota(jnp.int32, (blk, blk), 0)
    col = lax.broadcasted_iota(jnp.int32, (blk, blk), 1)
    w = jnp.where(col // CHUNK <= row // CHUNK, ws_ref[0], 0.0).astype(BF16)
    bs = bs_ref[0]
    for nb in range(tm // blk):
        rows = slice(nb * blk, (nb + 1) * blk)
        mixed = _dot(w, vn[rows]) + bs
        sg_ref[rows, :] = (u[rows] * mixed).astype(BF16)

    glu_ref[...] = proj(wa_ref, ba_ref) * _sigmoid(proj(wb_ref, bb_ref))
    ga_ref[...] = _sigmoid(proj(wga_ref, bga_ref))
    gb_ref[...] = _sigmoid(proj(wgb_ref, bgb_ref))


def _inproj(xb, w_in, b_in, ln_g, ln_b, ws, bs_b, *, tm, tn):
    t, d = xb.shape
    nseg = w_in.shape[1] // d
    ncol = d // tn
    w_specs = [pl.BlockSpec((d, tn), functools.partial(lambda i, j, s: (0, s * ncol + j), s=s))
               for s in range(nseg)]
    b_specs = [pl.BlockSpec((1, tn), functools.partial(lambda i, j, s: (0, s * ncol + j), s=s))
               for s in range(nseg)]
    vec = pl.BlockSpec((1, tn), lambda i, j: (0, j))
    tile = pl.BlockSpec((tm, tn), lambda i, j: (i, j))
    return pl.pallas_call(
        _inproj_kernel,
        grid=(t // tm, ncol),
        in_specs=[pl.BlockSpec((tm, d), lambda i, j: (i, 0))] + w_specs + b_specs + [
            vec, vec,
            pl.BlockSpec((1,) + ws.shape[1:], lambda i, j: (j, 0, 0)),
            pl.BlockSpec((1,) + bs_b.shape[1:], lambda i, j: (j, 0, 0)),
        ],
        out_specs=[tile, tile, tile, tile],
        out_shape=[jax.ShapeDtypeStruct((t, d), BF16), jax.ShapeDtypeStruct((t, d), F32),
                   jax.ShapeDtypeStruct((t, d), F32), jax.ShapeDtypeStruct((t, d), F32)],
        compiler_params=_params("parallel", "arbitrary"),
        name="inproj",
    )(xb, *([w_in] * nseg), *([b_in] * nseg), ln_g, ln_b, ws, bs_b)


def _mix_kernel(sg_ref, glu_ref, halo_ref, cw_ref, cb_ref, clg_ref, clb_ref,
                wa_ref, wb_ref, ga_ref, gb_ref, m_ref, buf_ref, c_ref, cn_ref, *, blocks_per_seq):
    i = pl.program_id(0)
    j = pl.program_id(1)
    tm, d = glu_ref.shape
    taps = cw_ref.shape[0]

    @pl.when(j == 0)
    def _():
        keep = (i % blocks_per_seq != 0).astype(F32)
        buf_ref[0:HALO, :] = halo_ref[...] * keep
        buf_ref[HALO:, :] = glu_ref[...]

        def cols(c, carry):
            cs = pl.ds(pl.multiple_of(c * LANES, LANES), LANES)
            acc = jnp.broadcast_to(cb_ref[:, cs], (tm, LANES))
            for k in range(taps):
                off = HALO - (taps - 1) + k
                acc = acc + cw_ref[k:k + 1, cs] * buf_ref[off:off + tm, cs]
            c_ref[:, cs] = acc
            return carry

        lax.fori_loop(0, d // LANES, cols, 0)
        c = _layer_norm(c_ref[...], clg_ref[...], clb_ref[...])
        cn_ref[...] = (c * _sigmoid(c)).astype(BF16)

    ya = _dot(sg_ref[...], wa_ref[...])
    yb = _dot(cn_ref[...], wb_ref[...])
    m_ref[...] = (ga_ref[...] * ya + gb_ref[...] * yb).astype(BF16)


def _mix(sg, glu, cv_w, cv_b, cv_ln_g, cv_ln_b, wa, wb, ga, gb, *, seq, tm, tn):
    t, d = glu.shape
    assert seq % tm == 0 and tm % HALO == 0 and cv_w.shape[0] - 1 <= HALO
    hb = tm // HALO
    row = pl.BlockSpec((tm, d), lambda i, j: (i, 0))
    vec = pl.BlockSpec((1, d), lambda i, j: (0, 0))
    wcol = pl.BlockSpec((d, tn), lambda i, j: (0, j))
    tile = pl.BlockSpec((tm, tn), lambda i, j: (i, j))
    return pl.pallas_call(
        functools.partial(_mix_kernel, blocks_per_seq=seq // tm),
        grid=(t // tm, d // tn),
        in_specs=[row, row,
                  pl.BlockSpec((HALO, d), lambda i, j: (jnp.maximum(i * hb - 1, 0), 0)),
                  pl.BlockSpec(cv_w.shape, lambda i, j: (0, 0)),
                  vec, vec, vec, wcol, wcol, tile, tile],
        out_specs=tile,
        out_shape=jax.ShapeDtypeStruct((t, d), BF16),
        scratch_shapes=[pltpu.VMEM((tm + HALO, d), F32), pltpu.VMEM((tm, d), F32),
                        pltpu.VMEM((tm, d), BF16)],
        compiler_params=_params("parallel", "arbitrary"),
        name="branch_mix",
    )(sg, glu, glu, cv_w, cv_b, cv_ln_g, cv_ln_b, wa, wb, ga, gb)


def _oproj_kernel(m_ref, x_ref, wo_ref, g_ref, b_ref, x1_ref, x1b_ref, *, alpha):
    mix = _dot(m_ref[...], wo_ref[...])
    x1 = _layer_norm(alpha * x_ref[...] + mix, g_ref[...], b_ref[...])
    x1_ref[...] = x1
    x1b_ref[...] = x1.astype(BF16)


def _oproj(m, x, wo, g, b, *, alpha, tm):
    t, d = x.shape
    row = pl.BlockSpec((tm, d), lambda i: (i, 0))
    vec = pl.BlockSpec((1, d), lambda i: (0, 0))
    return pl.pallas_call(
        functools.partial(_oproj_kernel, alpha=alpha),
        grid=(t // tm,),
        in_specs=[row, row, pl.BlockSpec((d, d), lambda i: (0, 0)), vec, vec],
        out_specs=[row, row],
        out_shape=[jax.ShapeDtypeStruct((t, d), F32), jax.ShapeDtypeStruct((t, d), BF16)],
        compiler_params=_params("parallel"),
        name="out_proj_ln1",
    )(m, x, wo, g, b)


def _top_k_rows(s, k):
    n = s.shape[0]
    idx = lax.broadcasted_iota(jnp.int32, s.shape, 0).astype(F32)
    rank = jnp.full(s.shape, float(k), F32)
    work = s
    tops = []
    for r in range(k):
        m = jnp.max(work, axis=0, keepdims=True)
        first = jnp.min(jnp.where(work == m, idx, float(n)), axis=0, keepdims=True)
        sel = idx == first
        rank = jnp.where(sel, float(r), rank)
        work = jnp.where(sel, -jnp.inf, work)
        tops.append(m)
    return rank, jnp.concatenate(tops, axis=0)


def _merge_top_k(t1, t2, k):
    aidx = lax.broadcasted_iota(jnp.int32, t1.shape, 0).astype(F32)
    nb = jnp.zeros(t1.shape, F32)
    head = t1 + t2[0:1]
    top = head[0:1]
    z = jnp.zeros_like(top)
    for _ in range(k):
        m = jnp.max(head, axis=0, keepdims=True)
        first = jnp.min(jnp.where(head == m, aidx, float(k)), axis=0, keepdims=True)
        sel = aidx == first
        z = z + jnp.exp(m - top)
        nb = jnp.where(sel, nb + 1.0, nb)
        nxt = jnp.sum(jnp.where(sel, nb, 0.0), axis=0, keepdims=True)
        t2n = jnp.sum(jnp.where(aidx == nxt, t2, 0.0), axis=0, keepdims=True)
        t2n = jnp.where(nxt >= float(k), -jnp.inf, t2n)
        head = jnp.where(sel, t1 + t2n, head)
    return nb, z


def _route_kernel(x_ref, wq_ref, k1_ref, k2_ref, r2_ref, b2_ref, nb1_ref, a1_ref, q_ref):
    tc = x_ref.shape[0]
    half = k1_ref.shape[-1]
    k = PEER_TOPK
    q_ref[...] = _dot(x_ref[...], wq_ref[...])
    k1 = k1_ref[0, 0].astype(BF16)
    k2 = k2_ref[0, 0].astype(BF16)
    nt = (((1,), (1,)), ((), ()))

    def chunk(c, carry):
        ts = pl.ds(pl.multiple_of(c * LANES, LANES), LANES)
        qc = q_ref[ts, :].astype(BF16)
        s1 = lax.dot_general(k1, qc[:, :half], nt, preferred_element_type=F32)
        s2 = lax.dot_general(k2, qc[:, half:], nt, preferred_element_type=F32)
        rank1, t1 = _top_k_rows(s1, k)
        rank2, t2 = _top_k_rows(s2, k)
        nb, z = _merge_top_k(t1, t2, k)
        nb1 = jnp.zeros_like(rank1)
        for a in range(k):
            nb1 = jnp.where(rank1 == float(a), nb[a:a + 1], nb1)
        r2_ref[0, :, ts] = rank2
        b2_ref[0, :, ts] = jnp.exp(s2 - t2[0:1])
        nb1_ref[0, :, ts] = nb1
        a1_ref[0, :, ts] = jnp.exp(s1 - t1[0:1]) / z
        return carry

    lax.fori_loop(0, tc // LANES, chunk, 0)


def _route(x1b, wq, keys, *, tc):
    t, d = x1b.shape
    heads, _, nkeys, half = keys.shape
    out = pl.BlockSpec((1, nkeys, tc), lambda i, h: (h, 0, i))
    shape = jax.ShapeDtypeStruct((heads, nkeys, t), F32)
    return pl.pallas_call(
        _route_kernel,
        grid=(t // tc, heads),
        in_specs=[pl.BlockSpec((tc, d), lambda i, h: (i, 0)),
                  pl.BlockSpec((d, 2 * half), lambda i, h: (0, h)),
                  pl.BlockSpec((1, 1, nkeys, half), lambda i, h: (h, 0, 0, 0)),
                  pl.BlockSpec((1, 1, nkeys, half), lambda i, h: (h, 1, 0, 0))],
        out_specs=[out, out, out, out],
        out_shape=[shape, shape, shape, shape],
        scratch_shapes=[pltpu.VMEM((tc, 2 * half), F32)],
        compiler_params=_params("parallel", "arbitrary"),
        name="peer_route",
    )(x1b, wq, keys, keys)


def _experts_kernel(x_ref, ut_ref, v_ref, r2_ref, b2_ref, nb1_ref, a1_ref, y_ref, *, rows_per_step):
    s = pl.program_id(1)
    heads, nkeys, tm = r2_ref.shape
    h = _dot(x_ref[...], ut_ref[...])
    ws = []
    for r in range(rows_per_step):
        i1 = s * rows_per_step + r
        wt = jnp.zeros((nkeys, tm), F32)
        for hd in range(heads):
            nb_row = nb1_ref[hd, pl.ds(i1, 1), :]
            a_row = a1_ref[hd, pl.ds(i1, 1), :]
            wt = wt + jnp.where(r2_ref[hd] < nb_row, b2_ref[hd], 0.0) * a_row
        ws.append(wt.T)
    w = jnp.concatenate(ws, axis=1)
    act = (w * _gelu(h)).astype(BF16)
    y = _dot(act, v_ref[...])

    @pl.when(s == 0)
    def _():
        y_ref[...] = y

    @pl.when(s != 0)
    def _():
        y_ref[...] += y


def _experts(x1b, ut, vb, r2, b2, nb1, a1, *, tm, rows_per_step):
    t, d = x1b.shape
    heads, nkeys, _ = r2.shape
    eb = rows_per_step * nkeys
    route = pl.BlockSpec((heads, nkeys, tm), lambda i, s: (0, 0, i))
    return pl.pallas_call(
        functools.partial(_experts_kernel, rows_per_step=rows_per_step),
        grid=(t // tm, nkeys // rows_per_step),
        in_specs=[pl.BlockSpec((tm, d), lambda i, s: (i, 0)),
                  pl.BlockSpec((d, eb), lambda i, s: (0, s)),
                  pl.BlockSpec((eb, d), lambda i, s: (s, 0)),
                  route, route, route, route],
        out_specs=pl.BlockSpec((tm, d), lambda i, s: (i, 0)),
        out_shape=jax.ShapeDtypeStruct((t, d), F32),
        compiler_params=_params("parallel", "arbitrary"),
        name="peer_experts",
    )(x1b, ut, vb, r2, b2, nb1, a1)


def _tail_kernel(x1_ref, y_ref, p_ref, g_ref, b_ref, wg_ref, wp_ref, o_ref, *, alpha):
    x2 = _layer_norm(alpha * x1_ref[...] + y_ref[...], g_ref[...], b_ref[...])
    gate = _sigmoid(_dot(x2.astype(BF16), wg_ref[...]))
    proj = _dot(p_ref[...].astype(BF16), wp_ref[...])
    o_ref[...] = x2 + gate * proj


def _tail(x1, y, p, g, b, wg, wp, *, alpha, tm):
    t, d = x1.shape
    pd = p.shape[1]
    row = pl.BlockSpec((tm, d), lambda i: (i, 0))
    vec = pl.BlockSpec((1, d), lambda i: (0, 0))
    return pl.pallas_call(
        functools.partial(_tail_kernel, alpha=alpha),
        grid=(t // tm,),
        in_specs=[row, row, pl.BlockSpec((tm, pd), lambda i: (i, 0)), vec, vec,
                  pl.BlockSpec((d, d), lambda i: (0, 0)), pl.BlockSpec((pd, d), lambda i: (0, 0))],
        out_specs=row,
        out_shape=jax.ShapeDtypeStruct((t, d), F32),
        compiler_params=_params("parallel"),
        name="ln2_ple",
    )(x1, y, p, g, b, wg, wp)


def kernel(x, p, w_in, b_in, gm_ln_g, gm_ln_b, gm_ws, gm_bs, w_gm_out, cv_w, cv_b, cv_ln_g, cv_ln_b,
           w_cv_out, w_o, ln1_g, ln1_b, peer_wq, peer_keys, peer_u, peer_v, ln2_g, ln2_b,
           ple_w_gate, ple_w_proj):
    bsz, seq, d = x.shape
    depth = w_in.shape[0]
    t = bsz * seq
    alpha = (2.0 * depth) ** 0.25
    groups, blk, _ = gm_ws.shape[1:]
    gdim = d // groups
    tm = min(512, seq)

    xf = x.reshape(t, d)
    for i in range(depth):
        row = lambda a: a[i].reshape(1, -1)
        bs_b = jnp.broadcast_to(gm_bs[i][:, :, None], (groups, blk, gdim))
        sg, glu, ga, gb = _inproj(xf.astype(BF16), w_in[i].astype(BF16), row(b_in), row(gm_ln_g), row(gm_ln_b),
                                  gm_ws[i], bs_b, tm=min(1024, t), tn=gdim)
        m = _mix(sg, glu, cv_w[i], row(cv_b), row(cv_ln_g), row(cv_ln_b),
                 w_gm_out[i].astype(BF16), w_cv_out[i].astype(BF16), ga, gb, seq=seq, tm=tm, tn=512)
        x1, x1b = _oproj(m, xf, w_o[i].astype(BF16), row(ln1_g), row(ln1_b), alpha=alpha, tm=tm)
        r2, b2, nb1, a1 = _route(x1b, peer_wq[i].astype(BF16), peer_keys[i], tc=min(512, t))
        y = _experts(x1b, peer_u[i].astype(BF16).T, peer_v[i].astype(BF16), r2, b2, nb1, a1,
                     tm=tm, rows_per_step=2)
        xf = _tail(x1, y, p[i].reshape(t, -1), row(ln2_g), row(ln2_b),
                   ple_w_gate[i].astype(BF16), ple_w_proj[i].astype(BF16), alpha=alpha, tm=tm)
    return xf.reshape(bsz, seq, d)
```

```python
import functools

import jax
import jax.numpy as jnp
from jax import lax
from jax.experimental import pallas as pl
from jax.experimental.pallas import tpu as pltpu

LN_EPS = 1e-5
CHUNK = 64
PEER_TOPK = 16
SQRT_HALF = 0.7071067811865476
LANES = 128
TOKEN_CHAIN = 512
ROUTE_CHUNK = 256
HALO = 32
VMEM_LIMIT = 56 * 1024 * 1024

F32 = jnp.float32
BF16 = jnp.bfloat16


def _gelu(x):
    return 0.5 * x * (1.0 + lax.erf(x * SQRT_HALF))


def _sigmoid(x):
    return 1.0 / (1.0 + jnp.exp(-x))


def _layer_norm(x, g, b):
    mu = jnp.mean(x, axis=-1, keepdims=True)
    d = x - mu
    var = jnp.mean(d * d, axis=-1, keepdims=True)
    return d * lax.rsqrt(var + LN_EPS) * g + b


def _dot(a, b):
    return jnp.dot(a, b, preferred_element_type=F32)


def _params(*sem):
    return pltpu.CompilerParams(dimension_semantics=sem, vmem_limit_bytes=VMEM_LIMIT)


def _inproj_kernel(x_ref, wu_ref, wv_ref, wa_ref, wb_ref, wga_ref, wgb_ref,
                   bu_ref, bv_ref, ba_ref, bb_ref, bga_ref, bgb_ref,
                   lng_ref, lnb_ref, ws_ref, bs_ref,
                   sg_ref, glu_ref, ga_ref, gb_ref):
    x = x_ref[...]
    tm = x.shape[0]
    blk = ws_ref.shape[1]

    def proj(w_ref, b_ref):
        return _dot(x, w_ref[...]) + b_ref[...]

    u = _gelu(proj(wu_ref, bu_ref))
    v = _gelu(proj(wv_ref, bv_ref))
    vn = _layer_norm(v, lng_ref[...], lnb_ref[...]).astype(BF16)
    row = lax.broadcasted_iota(jnp.int32, (blk, blk), 0)
    col = lax.broadcasted_iota(jnp.int32, (blk, blk), 1)
    w = jnp.where(col // CHUNK <= row // CHUNK, ws_ref[0], 0.0).astype(BF16)
    bs = bs_ref[0]
    for nb in range(tm // blk):
        rows = slice(nb * blk, (nb + 1) * blk)
        mixed = _dot(w, vn[rows]) + bs
        sg_ref[rows, :] = (u[rows] * mixed).astype(BF16)

    glu_ref[...] = proj(wa_ref, ba_ref) * _sigmoid(proj(wb_ref, bb_ref))
    ga_ref[...] = _sigmoid(proj(wga_ref, bga_ref))
    gb_ref[...] = _sigmoid(proj(wgb_ref, bgb_ref))


def _inproj(xb, w_in, b_in, ln_g, ln_b, ws, bs_b, *, tm, tn):
    t, d = xb.shape
    nseg = w_in.shape[1] // d
    ncol = d // tn
    w_specs = [pl.BlockSpec((d, tn), functools.partial(lambda i, j, s: (0, s * ncol + j), s=s))
               for s in range(nseg)]
    b_specs = [pl.BlockSpec((1, tn), functools.partial(lambda i, j, s: (0, s * ncol + j), s=s))
               for s in range(nseg)]
    vec = pl.BlockSpec((1, tn), lambda i, j: (0, j))
    tile = pl.BlockSpec((tm, tn), lambda i, j: (i, j))
    return pl.pallas_call(
        _inproj_kernel,
        grid=(t // tm, ncol),
        in_specs=[pl.BlockSpec((tm, d), lambda i, j: (i, 0))] + w_specs + b_specs + [
            vec, vec,
            pl.BlockSpec((1,) + ws.shape[1:], lambda i, j: (j, 0, 0)),
            pl.BlockSpec((1,) + bs_b.shape[1:], lambda i, j: (j, 0, 0)),
        ],
        out_specs=[tile, tile, tile, tile],
        out_shape=[jax.ShapeDtypeStruct((t, d), BF16), jax.ShapeDtypeStruct((t, d), F32),
                   jax.ShapeDtypeStruct((t, d), F32), jax.ShapeDtypeStruct((t, d), F32)],
        compiler_params=_params("parallel", "arbitrary"),
        name="inproj",
    )(xb, *([w_in] * nseg), *([b_in] * nseg), ln_g, ln_b, ws, bs_b)


def _mix_kernel(sg_ref, glu_ref, halo_ref, cw_ref, cb_ref, clg_ref, clb_ref,
                wa_ref, wb_ref, ga_ref, gb_ref, m_ref, buf_ref, c_ref, cn_ref, *, blocks_per_seq):
    i = pl.program_id(0)
    j = pl.program_id(1)
    tm, d = glu_ref.shape
    taps = cw_ref.shape[0]

    @pl.when(j == 0)
    def _():
        keep = (i % blocks_per_seq != 0).astype(F32)
        buf_ref[0:HALO, :] = halo_ref[...] * keep
        buf_ref[HALO:, :] = glu_ref[...]

        def cols(c, carry):
            cs = pl.ds(pl.multiple_of(c * LANES, LANES), LANES)
            acc = jnp.broadcast_to(cb_ref[:, cs], (tm, LANES))
            for k in range(taps):
                off = HALO - (taps - 1) + k
                acc = acc + cw_ref[k:k + 1, cs] * buf_ref[off:off + tm, cs]
            c_ref[:, cs] = acc
            return carry

        lax.fori_loop(0, d // LANES, cols, 0)
        c = _layer_norm(c_ref[...], clg_ref[...], clb_ref[...])
        cn_ref[...] = (c * _sigmoid(c)).astype(BF16)

    ya = _dot(sg_ref[...], wa_ref[...])
    yb = _dot(cn_ref[...], wb_ref[...])
    m_ref[...] = (ga_ref[...] * ya + gb_ref[...] * yb).astype(BF16)


def _mix(sg, glu, cv_w, cv_b, cv_ln_g, cv_ln_b, wa, wb, ga, gb, *, seq, tm, tn):
    t, d = glu.shape
    assert seq % tm == 0 and tm % HALO == 0 and cv_w.shape[0] - 1 <= HALO
    hb = tm // HALO
    row = pl.BlockSpec((tm, d), lambda i, j: (i, 0))
    vec = pl.BlockSpec((1, d), lambda i, j: (0, 0))
    wcol = pl.BlockSpec((d, tn), lambda i, j: (0, j))
    tile = pl.BlockSpec((tm, tn), lambda i, j: (i, j))
    return pl.pallas_call(
        functools.partial(_mix_kernel, blocks_per_seq=seq // tm),
        grid=(t // tm, d // tn),
        in_specs=[row, row,
                  pl.BlockSpec((HALO, d), lambda i, j: (jnp.maximum(i * hb - 1, 0), 0)),
                  pl.BlockSpec(cv_w.shape, lambda i, j: (0, 0)),
                  vec, vec, vec, wcol, wcol, tile, tile],
        out_specs=tile,
        out_shape=jax.ShapeDtypeStruct((t, d), BF16),
        scratch_shapes=[pltpu.VMEM((tm + HALO, d), F32), pltpu.VMEM((tm, d), F32),
                        pltpu.VMEM((tm, d), BF16)],
        compiler_params=_params("parallel", "arbitrary"),
        name="branch_mix",
    )(sg, glu, glu, cv_w, cv_b, cv_ln_g, cv_ln_b, wa, wb, ga, gb)


def _oproj_kernel(m_ref, x_ref, wo_ref, g_ref, b_ref, x1_ref, x1b_ref, x1t_ref, *, alpha):
    mix = _dot(m_ref[...], wo_ref[...])
    x1 = _layer_norm(alpha * x_ref[...] + mix, g_ref[...], b_ref[...])
    x1_ref[...] = x1
    x1b_ref[...] = x1.astype(BF16)
    x1t_ref[...] = x1.T.astype(BF16)


def _oproj(m, x, wo, g, b, *, alpha, tm):
    t, d = x.shape
    row = pl.BlockSpec((tm, d), lambda i: (i, 0))
    vec = pl.BlockSpec((1, d), lambda i: (0, 0))
    return pl.pallas_call(
        functools.partial(_oproj_kernel, alpha=alpha),
        grid=(t // tm,),
        in_specs=[row, row, pl.BlockSpec((d, d), lambda i: (0, 0)), vec, vec],
        out_specs=[row, row, pl.BlockSpec((d, tm), lambda i: (0, i))],
        out_shape=[jax.ShapeDtypeStruct((t, d), F32), jax.ShapeDtypeStruct((t, d), BF16),
                   jax.ShapeDtypeStruct((d, t), BF16)],
        compiler_params=_params("parallel"),
        name="out_proj_ln1",
    )(m, x, wo, g, b)


def _top_k_rows(ss, k):
    n = ss[0].shape[0]
    idx = lax.broadcasted_iota(jnp.int32, ss[0].shape, 0).astype(F32)

    def extract(tie_break):
        ranks = [jnp.full(s.shape, float(k), F32) for s in ss]
        works = list(ss)
        tops = [[] for _ in ss]
        for r in range(k):
            for i in range(len(ss)):
                m = jnp.max(works[i], axis=0, keepdims=True)
                sel = works[i] == m
                if tie_break:
                    first = jnp.min(jnp.where(sel, idx, float(n)), axis=0, keepdims=True)
                    sel = idx == first
                ranks[i] = jnp.where(sel, float(r), ranks[i])
                works[i] = jnp.where(sel, -jnp.inf, works[i])
                tops[i].append(m)
        return tuple(ranks), tuple(jnp.concatenate(t, axis=0) for t in tops)

    ranks, tops = extract(False)
    exact = None
    for rank in ranks:
        picked = jnp.sum((rank < float(k)).astype(F32), axis=0, keepdims=True)
        ok = jnp.all(picked == float(k))
        exact = ok if exact is None else jnp.logical_and(exact, ok)
    return lax.cond(exact, lambda: (ranks, tops), lambda: extract(True))


def _merge_top_k(t1, t2, k):
    aidx = lax.broadcasted_iota(jnp.int32, t1.shape, 0).astype(F32)
    nb = jnp.zeros(t1.shape, F32)
    head = t1 + t2[0:1]
    top = head[0:1]
    z = jnp.zeros_like(top)
    for _ in range(k):
        m = jnp.max(head, axis=0, keepdims=True)
        first = jnp.min(jnp.where(head == m, aidx, float(k)), axis=0, keepdims=True)
        sel = aidx == first
        z = z + jnp.exp(m - top)
        nb = jnp.where(sel, nb + 1.0, nb)
        nxt = jnp.sum(jnp.where(sel, nb, 0.0), axis=0, keepdims=True)
        t2n = jnp.sum(jnp.where(aidx == nxt, t2, 0.0), axis=0, keepdims=True)
        t2n = jnp.where(nxt >= float(k), -jnp.inf, t2n)
        head = jnp.where(sel, t1 + t2n, head)
    return nb, z


def _route_kernel(x_ref, wq_ref, k1_ref, k2_ref, r2_ref, b2_ref, nb1_ref, a1_ref, q_ref):
    tc = x_ref.shape[0]
    half = k1_ref.shape[-1]
    k = PEER_TOPK
    q_ref[...] = _dot(x_ref[...], wq_ref[...])
    k1 = k1_ref[0, 0].astype(BF16)
    k2 = k2_ref[0, 0].astype(BF16)
    nt = (((1,), (1,)), ((), ()))

    def chunk(c, carry):
        ts = pl.ds(pl.multiple_of(c * ROUTE_CHUNK, ROUTE_CHUNK), ROUTE_CHUNK)
        qc = q_ref[ts, :].astype(BF16)
        s1 = lax.dot_general(k1, qc[:, :half], nt, preferred_element_type=F32)
        s2 = lax.dot_general(k2, qc[:, half:], nt, preferred_element_type=F32)
        (rank1, rank2), (t1, t2) = _top_k_rows((s1, s2), k)
        nb, z = _merge_top_k(t1, t2, k)
        nb1 = jnp.zeros_like(rank1)
        for a in range(k):
            nb1 = jnp.where(rank1 == float(a), nb[a:a + 1], nb1)
        r2_ref[0, :, ts] = rank2.astype(BF16)
        b2_ref[0, :, ts] = jnp.exp(s2 - t2[0:1]).astype(BF16)
        nb1_ref[0, :, ts] = nb1
        a1_ref[0, :, ts] = jnp.exp(s1 - t1[0:1]) / z
        return carry

    lax.fori_loop(0, tc // ROUTE_CHUNK, chunk, 0)


def _route(x1b, wq, keys, *, tc):
    t, d = x1b.shape
    heads, _, nkeys, half = keys.shape
    out = pl.BlockSpec((1, nkeys, tc), lambda i, h: (h, 0, i))
    shape = jax.ShapeDtypeStruct((heads, nkeys, t), F32)
    shape_bf16 = jax.ShapeDtypeStruct((heads, nkeys, t), BF16)
    return pl.pallas_call(
        _route_kernel,
        grid=(t // tc, heads),
        in_specs=[pl.BlockSpec((tc, d), lambda i, h: (i, 0)),
                  pl.BlockSpec((d, 2 * half), lambda i, h: (0, h)),
                  pl.BlockSpec((1, 1, nkeys, half), lambda i, h: (h, 0, 0, 0)),
                  pl.BlockSpec((1, 1, nkeys, half), lambda i, h: (h, 1, 0, 0))],
        out_specs=[out, out, out, out],
        out_shape=[shape_bf16, shape_bf16, shape, shape],
        scratch_shapes=[pltpu.VMEM((tc, 2 * half), F32)],
        compiler_params=_params("parallel", "arbitrary"),
        name="peer_route",
    )(x1b, wq, keys, keys)


def _experts_kernel(xt_ref, u_ref, vt_ref, r2_ref, b2_ref, nb1_ref, a1_ref, y_ref, acc_ref, *, rows_per_step):
    s = pl.program_id(1)
    heads = r2_ref.shape[0]

    @pl.when(s == 0)
    def _():
        acc_ref[...] = jnp.zeros_like(acc_ref)

    for c in range(xt_ref.shape[1] // TOKEN_CHAIN):
        ts = slice(c * TOKEN_CHAIN, (c + 1) * TOKEN_CHAIN)
        wts = []
        for r in range(rows_per_step):
            i1 = s * rows_per_step + r
            wt = None
            for hd in range(heads):
                nb_row = nb1_ref[hd, pl.ds(i1, 1), ts].astype(BF16)
                a_row = a1_ref[hd, pl.ds(i1, 1), ts].astype(BF16)
                zero = jnp.zeros((), BF16)
                term = jnp.where(r2_ref[hd, :, ts] < nb_row, b2_ref[hd, :, ts], zero) * a_row
                wt = term if wt is None else wt + term
            wts.append(wt)
        w = jnp.concatenate(wts, axis=0)
        ht = _dot(u_ref[...], xt_ref[:, ts])
        act = w * _gelu(ht).astype(BF16)
        acc_ref[:, ts] += _dot(vt_ref[...], act)

    @pl.when(s == pl.num_programs(1) - 1)
    def _():
        y_ref[...] = acc_ref[...].T


def _experts(x1t, ub, vt, r2, b2, nb1, a1, *, tm, rows_per_step):
    d, t = x1t.shape
    heads, nkeys, _ = r2.shape
    eb = rows_per_step * nkeys
    route = pl.BlockSpec((heads, nkeys, tm), lambda i, s: (0, 0, i))
    return pl.pallas_call(
        functools.partial(_experts_kernel, rows_per_step=rows_per_step),
        grid=(t // tm, nkeys // rows_per_step),
        in_specs=[pl.BlockSpec((d, tm), lambda i, s: (0, i)),
                  pl.BlockSpec((eb, d), lambda i, s: (s, 0)),
                  pl.BlockSpec((d, eb), lambda i, s: (0, s)),
                  route, route, route, route],
        out_specs=pl.BlockSpec((tm, d), lambda i, s: (i, 0)),
        out_shape=jax.ShapeDtypeStruct((t, d), F32),
        scratch_shapes=[pltpu.VMEM((d, tm), F32)],
        compiler_params=_params("parallel", "arbitrary"),
        name="peer_experts",
    )(x1t, ub, vt, r2, b2, nb1, a1)


def _tail_kernel(x1_ref, y_ref, p_ref, g_ref, b_ref, wg_ref, wp_ref, o_ref, *, alpha):
    x2 = _layer_norm(alpha * x1_ref[...] + y_ref[...], g_ref[...], b_ref[...])
    gate = _sigmoid(_dot(x2.astype(BF16), wg_ref[...]))
    proj = _dot(p_ref[...].astype(BF16), wp_ref[...])
    o_ref[...] = x2 + gate * proj


def _tail(x1, y, p, g, b, wg, wp, *, alpha, tm):
    t, d = x1.shape
    pd = p.shape[1]
    row = pl.BlockSpec((tm, d), lambda i: (i, 0))
    vec = pl.BlockSpec((1, d), lambda i: (0, 0))
    return pl.pallas_call(
        functools.partial(_tail_kernel, alpha=alpha),
        grid=(t // tm,),
        in_specs=[row, row, pl.BlockSpec((tm, pd), lambda i: (i, 0)), vec, vec,
                  pl.BlockSpec((d, d), lambda i: (0, 0)), pl.BlockSpec((pd, d), lambda i: (0, 0))],
        out_specs=row,
        out_shape=jax.ShapeDtypeStruct((t, d), F32),
        compiler_params=_params("parallel"),
        name="ln2_ple",
    )(x1, y, p, g, b, wg, wp)


def kernel(x, p, w_in, b_in, gm_ln_g, gm_ln_b, gm_ws, gm_bs, w_gm_out, cv_w, cv_b, cv_ln_g, cv_ln_b,
           w_cv_out, w_o, ln1_g, ln1_b, peer_wq, peer_keys, peer_u, peer_v, ln2_g, ln2_b,
           ple_w_gate, ple_w_proj):
    bsz, seq, d = x.shape
    depth = w_in.shape[0]
    t = bsz * seq
    alpha = (2.0 * depth) ** 0.25
    groups, blk, _ = gm_ws.shape[1:]
    gdim = d // groups
    tm = min(512, seq)

    xf = x.reshape(t, d)
    for i in range(depth):
        row = lambda a: a[i].reshape(1, -1)
        bs_b = jnp.broadcast_to(gm_bs[i][:, :, None], (groups, blk, gdim))
        sg, glu, ga, gb = _inproj(xf.astype(BF16), w_in[i].astype(BF16), row(b_in), row(gm_ln_g), row(gm_ln_b),
                                  gm_ws[i], bs_b, tm=min(1024, t), tn=gdim)
        m = _mix(sg, glu, cv_w[i], row(cv_b), row(cv_ln_g), row(cv_ln_b),
                 w_gm_out[i].astype(BF16), w_cv_out[i].astype(BF16), ga, gb, seq=seq, tm=tm, tn=512)
        x1, x1b, x1t = _oproj(m, xf, w_o[i].astype(BF16), row(ln1_g), row(ln1_b), alpha=alpha, tm=tm)
        r2, b2, nb1, a1 = _route(x1b, peer_wq[i].astype(BF16), peer_keys[i], tc=min(512, t))
        y = _experts(x1t, peer_u[i].astype(BF16), peer_v[i].astype(BF16).T, r2, b2, nb1, a1,
                     tm=tm, rows_per_step=8)
        xf = _tail(x1, y, p[i].reshape(t, -1), row(ln2_g), row(ln2_b),
                   ple_w_gate[i].astype(BF16), ple_w_proj[i].astype(BF16), alpha=alpha, tm=tm)
    return xf.reshape(bsz, seq, d)
```

```python
import functools

import jax
import jax.numpy as jnp
from jax import lax
from jax.experimental import pallas as pl
from jax.experimental.pallas import tpu as pltpu

LN_EPS = 1e-5
CHUNK = 64
PEER_TOPK = 16
SQRT_HALF = 0.7071067811865476
LANES = 128
SUBLANES = 8
EXPERT_SUB_ROWS = 2
ROUTE_CHUNK = 256
HALO = 32
VMEM_LIMIT = 56 * 1024 * 1024

F32 = jnp.float32
BF16 = jnp.bfloat16


def _gelu(x):
    return 0.5 * x * (1.0 + lax.erf(x * SQRT_HALF))


def _sigmoid(x):
    return 1.0 / (1.0 + jnp.exp(-x))


def _layer_norm(x, g, b):
    mu = jnp.mean(x, axis=-1, keepdims=True)
    d = x - mu
    var = jnp.mean(d * d, axis=-1, keepdims=True)
    return d * lax.rsqrt(var + LN_EPS) * g + b


def _dot(a, b):
    return jnp.dot(a, b, preferred_element_type=F32)


def _params(*sem):
    return pltpu.CompilerParams(dimension_semantics=sem, vmem_limit_bytes=VMEM_LIMIT)


def _inproj_kernel(x_ref, wu_ref, wv_ref, wa_ref, wb_ref, wga_ref, wgb_ref,
                   bu_ref, bv_ref, ba_ref, bb_ref, bga_ref, bgb_ref,
                   lng_ref, lnb_ref, ws_ref, bs_ref,
                   sg_ref, glu_ref, ga_ref, gb_ref):
    x = x_ref[...]
    tm = x.shape[0]
    blk = ws_ref.shape[1]

    def proj(w_ref, b_ref):
        return _dot(x, w_ref[...]) + b_ref[...]

    u = _gelu(proj(wu_ref, bu_ref))
    v = _gelu(proj(wv_ref, bv_ref))
    vn = _layer_norm(v, lng_ref[...], lnb_ref[...]).astype(BF16)
    row = lax.broadcasted_iota(jnp.int32, (blk, blk), 0)
    col = lax.broadcasted_iota(jnp.int32, (blk, blk), 1)
    w = jnp.where(col // CHUNK <= row // CHUNK, ws_ref[0], 0.0).astype(BF16)
    bs = bs_ref[0]
    for nb in range(tm // blk):
        rows = slice(nb * blk, (nb + 1) * blk)
        mixed = _dot(w, vn[rows]) + bs
        sg_ref[rows, :] = (u[rows] * mixed).astype(BF16)

    glu_ref[...] = proj(wa_ref, ba_ref) * _sigmoid(proj(wb_ref, bb_ref))
    ga_ref[...] = _sigmoid(proj(wga_ref, bga_ref))
    gb_ref[...] = _sigmoid(proj(wgb_ref, bgb_ref))


def _inproj(xb, w_in, b_in, ln_g, ln_b, ws, bs_b, *, tm, tn):
    t, d = xb.shape
    nseg = w_in.shape[1] // d
    ncol = d // tn
    w_specs = [pl.BlockSpec((d, tn), functools.partial(lambda i, j, s: (0, s * ncol + j), s=s))
               for s in range(nseg)]
    b_specs = [pl.BlockSpec((1, tn), functools.partial(lambda i, j, s: (0, s * ncol + j), s=s))
               for s in range(nseg)]
    vec = pl.BlockSpec((1, tn), lambda i, j: (0, j))
    tile = pl.BlockSpec((tm, tn), lambda i, j: (i, j))
    return pl.pallas_call(
        _inproj_kernel,
        grid=(t // tm, ncol),
        in_specs=[pl.BlockSpec((tm, d), lambda i, j: (i, 0))] + w_specs + b_specs + [
            vec, vec,
            pl.BlockSpec((1,) + ws.shape[1:], lambda i, j: (j, 0, 0)),
            pl.BlockSpec((1,) + bs_b.shape[1:], lambda i, j: (j, 0, 0)),
        ],
        out_specs=[tile, tile, tile, tile],
        out_shape=[jax.ShapeDtypeStruct((t, d), BF16), jax.ShapeDtypeStruct((t, d), F32),
                   jax.ShapeDtypeStruct((t, d), F32), jax.ShapeDtypeStruct((t, d), F32)],
        compiler_params=_params("parallel", "arbitrary"),
        name="inproj",
    )(xb, *([w_in] * nseg), *([b_in] * nseg), ln_g, ln_b, ws, bs_b)


def _mix_kernel(sg_ref, glu_ref, halo_ref, cw_ref, cb_ref, clg_ref, clb_ref,
                wa_ref, wb_ref, ga_ref, gb_ref, m_ref, buf_ref, c_ref, cn_ref, *, blocks_per_seq):
    i = pl.program_id(0)
    j = pl.program_id(1)
    tm, d = glu_ref.shape
    taps = cw_ref.shape[0]

    @pl.when(j == 0)
    def _():
        keep = (i % blocks_per_seq != 0).astype(F32)
        buf_ref[0:HALO, :] = halo_ref[...] * keep
        buf_ref[HALO:HALO + tm, :] = glu_ref[...]
        buf_ref[HALO + tm:, :] = jnp.zeros((SUBLANES, d), F32)
        base = HALO - (taps - 1)

        def cols(c, carry):
            cs = pl.ds(pl.multiple_of(c * LANES, LANES), LANES)
            acc = jnp.broadcast_to(cb_ref[:, cs], (tm, LANES))
            for sh in range(SUBLANES):
                part = None
                for k in range(taps):
                    off = base + k
                    if off % SUBLANES == sh:
                        term = cw_ref[k:k + 1, cs] * buf_ref[off - sh:off - sh + tm + SUBLANES, cs]
                        part = term if part is None else part + term
                if part is not None:
                    acc = acc + part[sh:sh + tm]
            c_ref[:, cs] = acc
            return carry

        lax.fori_loop(0, d // LANES, cols, 0)
        c = _layer_norm(c_ref[...], clg_ref[...], clb_ref[...])
        cn_ref[...] = (c * _sigmoid(c)).astype(BF16)

    ya = _dot(sg_ref[...], wa_ref[...])
    yb = _dot(cn_ref[...], wb_ref[...])
    m_ref[...] = (ga_ref[...] * ya + gb_ref[...] * yb).astype(BF16)


def _mix(sg, glu, cv_w, cv_b, cv_ln_g, cv_ln_b, wa, wb, ga, gb, *, seq, tm, tn):
    t, d = glu.shape
    assert seq % tm == 0 and tm % HALO == 0 and cv_w.shape[0] - 1 <= HALO
    hb = tm // HALO
    row = pl.BlockSpec((tm, d), lambda i, j: (i, 0))
    vec = pl.BlockSpec((1, d), lambda i, j: (0, 0))
    wcol = pl.BlockSpec((d, tn), lambda i, j: (0, j))
    tile = pl.BlockSpec((tm, tn), lambda i, j: (i, j))
    return pl.pallas_call(
        functools.partial(_mix_kernel, blocks_per_seq=seq // tm),
        grid=(t // tm, d // tn),
        in_specs=[row, row,
                  pl.BlockSpec((HALO, d), lambda i, j: (jnp.maximum(i * hb - 1, 0), 0)),
                  pl.BlockSpec(cv_w.shape, lambda i, j: (0, 0)),
                  vec, vec, vec, wcol, wcol, tile, tile],
        out_specs=tile,
        out_shape=jax.ShapeDtypeStruct((t, d), BF16),
        scratch_shapes=[pltpu.VMEM((HALO + tm + SUBLANES, d), F32), pltpu.VMEM((tm, d), F32),
                        pltpu.VMEM((tm, d), BF16)],
        compiler_params=_params("parallel", "arbitrary"),
        name="branch_mix",
    )(sg, glu, glu, cv_w, cv_b, cv_ln_g, cv_ln_b, wa, wb, ga, gb)


def _oproj_kernel(m_ref, x_ref, wo_ref, g_ref, b_ref, x1_ref, x1b_ref, x1t_ref, *, alpha):
    mix = _dot(m_ref[...], wo_ref[...])
    x1 = _layer_norm(alpha * x_ref[...] + mix, g_ref[...], b_ref[...])
    x1_ref[...] = x1
    x1b_ref[...] = x1.astype(BF16)
    x1t_ref[...] = x1.T.astype(BF16)


def _oproj(m, x, wo, g, b, *, alpha, tm):
    t, d = x.shape
    row = pl.BlockSpec((tm, d), lambda i: (i, 0))
    vec = pl.BlockSpec((1, d), lambda i: (0, 0))
    return pl.pallas_call(
        functools.partial(_oproj_kernel, alpha=alpha),
        grid=(t // tm,),
        in_specs=[row, row, pl.BlockSpec((d, d), lambda i: (0, 0)), vec, vec],
        out_specs=[row, row, pl.BlockSpec((d, tm), lambda i: (0, i))],
        out_shape=[jax.ShapeDtypeStruct((t, d), F32), jax.ShapeDtypeStruct((t, d), BF16),
                   jax.ShapeDtypeStruct((d, t), BF16)],
        compiler_params=_params("parallel"),
        name="out_proj_ln1",
    )(m, x, wo, g, b)


def _top_k_rows(ss, k):
    n = ss[0].shape[0]
    idx = lax.broadcasted_iota(jnp.int32, ss[0].shape, 0).astype(F32)

    def extract(tie_break):
        ranks = [jnp.full(s.shape, float(k), F32) for s in ss]
        works = list(ss)
        tops = [[] for _ in ss]
        for r in range(k):
            for i in range(len(ss)):
                m = jnp.max(works[i], axis=0, keepdims=True)
                sel = works[i] == m
                if tie_break:
                    first = jnp.min(jnp.where(sel, idx, float(n)), axis=0, keepdims=True)
                    sel = idx == first
                ranks[i] = jnp.where(sel, float(r), ranks[i])
                works[i] = jnp.where(sel, -jnp.inf, works[i])
                tops[i].append(m)
        return tuple(ranks), tuple(jnp.concatenate(t, axis=0) for t in tops)

    ranks, tops = extract(False)
    exact = None
    for rank in ranks:
        picked = jnp.sum((rank < float(k)).astype(F32), axis=0, keepdims=True)
        ok = jnp.all(picked == float(k))
        exact = ok if exact is None else jnp.logical_and(exact, ok)
    return lax.cond(exact, lambda: (ranks, tops), lambda: extract(True))


def _merge_top_k(t1, t2, k):
    aidx = lax.broadcasted_iota(jnp.int32, t1.shape, 0).astype(F32)
    nb = jnp.zeros(t1.shape, F32)
    head = t1 + t2[0:1]
    top = head[0:1]
    z = jnp.zeros_like(top)
    for _ in range(k):
        m = jnp.max(head, axis=0, keepdims=True)
        first = jnp.min(jnp.where(head == m, aidx, float(k)), axis=0, keepdims=True)
        sel = aidx == first
        z = z + jnp.exp(m - top)
        nb = jnp.where(sel, nb + 1.0, nb)
        nxt = jnp.sum(jnp.where(sel, nb, 0.0), axis=0, keepdims=True)
        t2n = jnp.sum(jnp.where(aidx == nxt, t2, 0.0), axis=0, keepdims=True)
        t2n = jnp.where(nxt >= float(k), -jnp.inf, t2n)
        head = jnp.where(sel, t1 + t2n, head)
    return nb, z


def _route_kernel(x_ref, wq_ref, k1_ref, k2_ref, r2_ref, b2_ref, nb1_ref, a1_ref, q_ref):
    tc = x_ref.shape[0]
    half = k1_ref.shape[-1]
    k = PEER_TOPK
    h = pl.program_id(1)

    @pl.when(h == 0)
    def _():
        q_ref[...] = _dot(x_ref[...], wq_ref[...]).astype(BF16)

    k1 = k1_ref[0, 0].astype(BF16)
    k2 = k2_ref[0, 0].astype(BF16)
    nt = (((1,), (1,)), ((), ()))
    hs = pl.ds(pl.multiple_of(h * 2 * half, 2 * half), 2 * half)

    def chunk(c, carry):
        ts = pl.ds(pl.multiple_of(c * ROUTE_CHUNK, ROUTE_CHUNK), ROUTE_CHUNK)
        qc = q_ref[ts, hs]
        s1 = lax.dot_general(k1, qc[:, :half], nt, preferred_element_type=F32)
        s2 = lax.dot_general(k2, qc[:, half:], nt, preferred_element_type=F32)
        (rank1, rank2), (t1, t2) = _top_k_rows((s1, s2), k)
        nb, z = _merge_top_k(t1, t2, k)
        nb1 = jnp.zeros_like(rank1)
        for a in range(k):
            nb1 = jnp.where(rank1 == float(a), nb[a:a + 1], nb1)
        r2_ref[0, :, ts] = rank2.astype(BF16)
        b2_ref[0, :, ts] = jnp.exp(s2 - t2[0:1]).astype(BF16)
        nb1_ref[0, :, ts] = nb1
        a1_ref[0, :, ts] = jnp.exp(s1 - t1[0:1]) / z
        return carry

    lax.fori_loop(0, tc // ROUTE_CHUNK, chunk, 0)


def _route(x1b, wq, keys, *, tc):
    t, d = x1b.shape
    heads, _, nkeys, half = keys.shape
    out = pl.BlockSpec((1, nkeys, tc), lambda i, h: (h, 0, i))
    shape = jax.ShapeDtypeStruct((heads, nkeys, t), F32)
    shape_bf16 = jax.ShapeDtypeStruct((heads, nkeys, t), BF16)
    return pl.pallas_call(
        _route_kernel,
        grid=(t // tc, heads),
        in_specs=[pl.BlockSpec((tc, d), lambda i, h: (i, 0)),
                  pl.BlockSpec(wq.shape, lambda i, h: (0, 0)),
                  pl.BlockSpec((1, 1, nkeys, half), lambda i, h: (h, 0, 0, 0)),
                  pl.BlockSpec((1, 1, nkeys, half), lambda i, h: (h, 1, 0, 0))],
        out_specs=[out, out, out, out],
        out_shape=[shape_bf16, shape_bf16, shape, shape],
        scratch_shapes=[pltpu.VMEM((tc, wq.shape[1]), BF16)],
        compiler_params=_params("parallel", "arbitrary"),
        name="peer_route",
    )(x1b, wq, keys, keys)


def _experts_kernel(xt_ref, u_ref, vt_ref, r2_ref, b2_ref, nb1_ref, a1_ref, y_ref, acc_ref, *, rows_per_step):
    s = pl.program_id(1)
    heads = r2_ref.shape[0]

    @pl.when(s == 0)
    def _():
        acc_ref[...] = jnp.zeros_like(acc_ref)

    nkeys = r2_ref.shape[1]
    sub = EXPERT_SUB_ROWS
    acts = []
    for q in range(rows_per_step // sub):
        wts = []
        for r in range(q * sub, (q + 1) * sub):
            i1 = s * rows_per_step + r
            wt = None
            for hd in range(heads):
                nb_row = nb1_ref[hd, pl.ds(i1, 1), :].astype(BF16)
                a_row = a1_ref[hd, pl.ds(i1, 1), :].astype(BF16)
                zero = jnp.zeros((), BF16)
                term = jnp.where(r2_ref[hd] < nb_row, b2_ref[hd], zero) * a_row
                wt = term if wt is None else wt + term
            wts.append(wt)
        w = jnp.concatenate(wts, axis=0)
        ht = _dot(u_ref[q * sub * nkeys:(q + 1) * sub * nkeys, :], xt_ref[...])
        acts.append(w * _gelu(ht).astype(BF16))
    act = jnp.concatenate(acts, axis=0)
    acc_ref[...] += _dot(vt_ref[...], act)

    @pl.when(s == pl.num_programs(1) - 1)
    def _():
        y_ref[...] = acc_ref[...].T


def _experts(x1t, ub, vt, r2, b2, nb1, a1, *, tm, rows_per_step):
    d, t = x1t.shape
    heads, nkeys, _ = r2.shape
    eb = rows_per_step * nkeys
    route = pl.BlockSpec((heads, nkeys, tm), lambda i, s: (0, 0, i))
    return pl.pallas_call(
        functools.partial(_experts_kernel, rows_per_step=rows_per_step),
        grid=(t // tm, nkeys // rows_per_step),
        in_specs=[pl.BlockSpec((d, tm), lambda i, s: (0, i)),
                  pl.BlockSpec((eb, d), lambda i, s: (s, 0)),
                  pl.BlockSpec((d, eb), lambda i, s: (0, s)),
                  route, route, route, route],
        out_specs=pl.BlockSpec((tm, d), lambda i, s: (i, 0)),
        out_shape=jax.ShapeDtypeStruct((t, d), F32),
        scratch_shapes=[pltpu.VMEM((d, tm), F32)],
        compiler_params=_params("parallel", "arbitrary"),
        name="peer_experts",
    )(x1t, ub, vt, r2, b2, nb1, a1)


def _transpose_cast_kernel(v_ref, vt_ref):
    vt_ref[...] = v_ref[...].T.astype(BF16)


def _transpose_cast(v, *, rows):
    n, d = v.shape
    return pl.pallas_call(
        _transpose_cast_kernel,
        grid=(n // rows,),
        in_specs=[pl.BlockSpec((rows, d), lambda i: (i, 0))],
        out_specs=pl.BlockSpec((d, rows), lambda i: (0, i)),
        out_shape=jax.ShapeDtypeStruct((d, n), BF16),
        compiler_params=_params("parallel"),
        name="expert_v_layout",
    )(v)


def _tail_kernel(x1_ref, y_ref, p_ref, g_ref, b_ref, wg_ref, wp_ref, o_ref, *, alpha):
    x2 = _layer_norm(alpha * x1_ref[...] + y_ref[...], g_ref[...], b_ref[...])
    gate = _sigmoid(_dot(x2.astype(BF16), wg_ref[...]))
    proj = _dot(p_ref[...].astype(BF16), wp_ref[...])
    o_ref[...] = x2 + gate * proj


def _tail(x1, y, p, g, b, wg, wp, *, alpha, tm):
    t, d = x1.shape
    pd = p.shape[1]
    row = pl.BlockSpec((tm, d), lambda i: (i, 0))
    vec = pl.BlockSpec((1, d), lambda i: (0, 0))
    return pl.pallas_call(
        functools.partial(_tail_kernel, alpha=alpha),
        grid=(t // tm,),
        in_specs=[row, row, pl.BlockSpec((tm, pd), lambda i: (i, 0)), vec, vec,
                  pl.BlockSpec((d, d), lambda i: (0, 0)), pl.BlockSpec((pd, d), lambda i: (0, 0))],
        out_specs=row,
        out_shape=jax.ShapeDtypeStruct((t, d), F32),
        compiler_params=_params("parallel"),
        name="ln2_ple",
    )(x1, y, p, g, b, wg, wp)


def kernel(x, p, w_in, b_in, gm_ln_g, gm_ln_b, gm_ws, gm_bs, w_gm_out, cv_w, cv_b, cv_ln_g, cv_ln_b,
           w_cv_out, w_o, ln1_g, ln1_b, peer_wq, peer_keys, peer_u, peer_v, ln2_g, ln2_b,
           ple_w_gate, ple_w_proj):
    bsz, seq, d = x.shape
    depth = w_in.shape[0]
    t = bsz * seq
    alpha = (2.0 * depth) ** 0.25
    groups, blk, _ = gm_ws.shape[1:]
    gdim = d // groups
    tm = min(512, seq)

    xf = x.reshape(t, d)
    for i in range(depth):
        row = lambda a: a[i].reshape(1, -1)
        bs_b = jnp.broadcast_to(gm_bs[i][:, :, None], (groups, blk, gdim))
        sg, glu, ga, gb = _inproj(xf.astype(BF16), w_in[i].astype(BF16), row(b_in), row(gm_ln_g), row(gm_ln_b),
                                  gm_ws[i], bs_b, tm=min(1024, t), tn=gdim)
        m = _mix(sg, glu, cv_w[i], row(cv_b), row(cv_ln_g), row(cv_ln_b),
                 w_gm_out[i].astype(BF16), w_cv_out[i].astype(BF16), ga, gb, seq=seq, tm=tm, tn=512)
        x1, x1b, x1t = _oproj(m, xf, w_o[i].astype(BF16), row(ln1_g), row(ln1_b), alpha=alpha, tm=tm)
        r2, b2, nb1, a1 = _route(x1b, peer_wq[i].astype(BF16), peer_keys[i], tc=min(512, t))
        y = _experts(x1t, peer_u[i].astype(BF16), _transpose_cast(peer_v[i], rows=512), r2, b2, nb1, a1,
                     tm=tm, rows_per_step=8)
        xf = _tail(x1, y, p[i].reshape(t, -1), row(ln2_g), row(ln2_b),
                   ple_w_gate[i].astype(BF16), ple_w_proj[i].astype(BF16), alpha=alpha, tm=tm)
    return xf.reshape(bsz, seq, d)
```

```python
import functools

import jax
import jax.numpy as jnp
from jax import lax
from jax.experimental import pallas as pl
from jax.experimental.pallas import tpu as pltpu

LN_EPS = 1e-5
CHUNK = 64
PEER_TOPK = 16
SQRT_HALF = 0.7071067811865476
LANES = 128
SUBLANES = 8
EXPERT_SUB_ROWS = 2
ROUTE_CHUNK = 256
HALO = 32
VMEM_LIMIT = 56 * 1024 * 1024

F32 = jnp.float32
BF16 = jnp.bfloat16


def _gelu(x):
    return 0.5 * x * (1.0 + lax.erf(x * SQRT_HALF))


def _sigmoid(x):
    return 1.0 / (1.0 + jnp.exp(-x))


def _layer_norm(x, g, b):
    mu = jnp.mean(x, axis=-1, keepdims=True)
    d = x - mu
    var = jnp.mean(d * d, axis=-1, keepdims=True)
    return d * lax.rsqrt(var + LN_EPS) * g + b


def _dot(a, b):
    return jnp.dot(a, b, preferred_element_type=F32)


def _params(*sem):
    return pltpu.CompilerParams(dimension_semantics=sem, vmem_limit_bytes=VMEM_LIMIT)


def _inproj_kernel(x_ref, wu_ref, wv_ref, wa_ref, wb_ref, wga_ref, wgb_ref,
                   bu_ref, bv_ref, ba_ref, bb_ref, bga_ref, bgb_ref,
                   lng_ref, lnb_ref, ws_ref, bs_ref,
                   sg_ref, glu_ref, ga_ref, gb_ref, xb_ref):
    @pl.when(pl.program_id(1) == 0)
    def _():
        xb_ref[...] = x_ref[...].astype(BF16)

    x = xb_ref[...]
    tm = x.shape[0]
    blk = ws_ref.shape[1]

    def proj(w_ref, b_ref):
        return _dot(x, w_ref[...]) + b_ref[...]

    u = _gelu(proj(wu_ref, bu_ref))
    v = _gelu(proj(wv_ref, bv_ref))
    vn = _layer_norm(v, lng_ref[...], lnb_ref[...]).astype(BF16)
    row = lax.broadcasted_iota(jnp.int32, (blk, blk), 0)
    col = lax.broadcasted_iota(jnp.int32, (blk, blk), 1)
    w = jnp.where(col // CHUNK <= row // CHUNK, ws_ref[0], 0.0).astype(BF16)
    bs = bs_ref[0]
    for nb in range(tm // blk):
        rows = slice(nb * blk, (nb + 1) * blk)
        mixed = _dot(w, vn[rows]) + bs
        sg_ref[rows, :] = (u[rows] * mixed).astype(BF16)

    glu_ref[...] = proj(wa_ref, ba_ref) * _sigmoid(proj(wb_ref, bb_ref))
    ga_ref[...] = _sigmoid(proj(wga_ref, bga_ref))
    gb_ref[...] = _sigmoid(proj(wgb_ref, bgb_ref))


def _inproj(x, w_in, b_in, ln_g, ln_b, ws, bs_b, *, tm, tn):
    t, d = x.shape
    nseg = w_in.shape[1] // d
    ncol = d // tn
    w_specs = [pl.BlockSpec((d, tn), functools.partial(lambda i, j, s: (0, s * ncol + j), s=s))
               for s in range(nseg)]
    b_specs = [pl.BlockSpec((1, tn), functools.partial(lambda i, j, s: (0, s * ncol + j), s=s))
               for s in range(nseg)]
    vec = pl.BlockSpec((1, tn), lambda i, j: (0, j))
    tile = pl.BlockSpec((tm, tn), lambda i, j: (i, j))
    return pl.pallas_call(
        _inproj_kernel,
        grid=(t // tm, ncol),
        in_specs=[pl.BlockSpec((tm, d), lambda i, j: (i, 0))] + w_specs + b_specs + [
            vec, vec,
            pl.BlockSpec((1,) + ws.shape[1:], lambda i, j: (j, 0, 0)),
            pl.BlockSpec((1,) + bs_b.shape[1:], lambda i, j: (j, 0, 0)),
        ],
        out_specs=[tile, tile, tile, tile],
        out_shape=[jax.ShapeDtypeStruct((t, d), BF16), jax.ShapeDtypeStruct((t, d), F32),
                   jax.ShapeDtypeStruct((t, d), F32), jax.ShapeDtypeStruct((t, d), F32)],
        scratch_shapes=[pltpu.VMEM((tm, d), BF16)],
        compiler_params=_params("parallel", "arbitrary"),
        name="inproj",
    )(x, *([w_in] * nseg), *([b_in] * nseg), ln_g, ln_b, ws, bs_b)


def _mix_kernel(sg_ref, glu_ref, halo_ref, cw_ref, cb_ref, clg_ref, clb_ref,
                wa_ref, wb_ref, ga_ref, gb_ref, m_ref, buf_ref, c_ref, cn_ref, *, blocks_per_seq):
    i = pl.program_id(0)
    j = pl.program_id(1)
    tm, d = glu_ref.shape
    taps = cw_ref.shape[0]

    @pl.when(j == 0)
    def _():
        keep = (i % blocks_per_seq != 0).astype(F32)
        buf_ref[0:HALO, :] = halo_ref[...] * keep
        buf_ref[HALO:HALO + tm, :] = glu_ref[...]
        buf_ref[HALO + tm:, :] = jnp.zeros((SUBLANES, d), F32)
        base = HALO - (taps - 1)

        def cols(c, carry):
            cs = pl.ds(pl.multiple_of(c * LANES, LANES), LANES)
            acc = jnp.broadcast_to(cb_ref[:, cs], (tm, LANES))
            for sh in range(SUBLANES):
                part = None
                for k in range(taps):
                    off = base + k
                    if off % SUBLANES == sh:
                        term = cw_ref[k:k + 1, cs] * buf_ref[off - sh:off - sh + tm + SUBLANES, cs]
                        part = term if part is None else part + term
                if part is not None:
                    acc = acc + part[sh:sh + tm]
            c_ref[:, cs] = acc
            return carry

        lax.fori_loop(0, d // LANES, cols, 0)
        c = _layer_norm(c_ref[...], clg_ref[...], clb_ref[...])
        cn_ref[...] = (c * _sigmoid(c)).astype(BF16)

    ya = _dot(sg_ref[...], wa_ref[...])
    yb = _dot(cn_ref[...], wb_ref[...])
    m_ref[...] = (ga_ref[...] * ya + gb_ref[...] * yb).astype(BF16)


def _mix(sg, glu, cv_w, cv_b, cv_ln_g, cv_ln_b, wa, wb, ga, gb, *, seq, tm, tn):
    t, d = glu.shape
    assert seq % tm == 0 and tm % HALO == 0 and cv_w.shape[0] - 1 <= HALO
    hb = tm // HALO
    row = pl.BlockSpec((tm, d), lambda i, j: (i, 0))
    vec = pl.BlockSpec((1, d), lambda i, j: (0, 0))
    wcol = pl.BlockSpec((d, tn), lambda i, j: (0, j))
    tile = pl.BlockSpec((tm, tn), lambda i, j: (i, j))
    return pl.pallas_call(
        functools.partial(_mix_kernel, blocks_per_seq=seq // tm),
        grid=(t // tm, d // tn),
        in_specs=[row, row,
                  pl.BlockSpec((HALO, d), lambda i, j: (jnp.maximum(i * hb - 1, 0), 0)),
                  pl.BlockSpec(cv_w.shape, lambda i, j: (0, 0)),
                  vec, vec, vec, wcol, wcol, tile, tile],
        out_specs=tile,
        out_shape=jax.ShapeDtypeStruct((t, d), BF16),
        scratch_shapes=[pltpu.VMEM((HALO + tm + SUBLANES, d), F32), pltpu.VMEM((tm, d), F32),
                        pltpu.VMEM((tm, d), BF16)],
        compiler_params=_params("parallel", "arbitrary"),
        name="branch_mix",
    )(sg, glu, glu, cv_w, cv_b, cv_ln_g, cv_ln_b, wa, wb, ga, gb)


def _oproj_kernel(m_ref, x_ref, wo_ref, g_ref, b_ref, x1_ref, x1b_ref, x1t_ref, *, alpha):
    mix = _dot(m_ref[...], wo_ref[...])
    x1 = _layer_norm(alpha * x_ref[...] + mix, g_ref[...], b_ref[...])
    x1_ref[...] = x1
    x1b_ref[...] = x1.astype(BF16)
    x1t_ref[...] = x1.T.astype(BF16)


def _oproj(m, x, wo, g, b, *, alpha, tm):
    t, d = x.shape
    row = pl.BlockSpec((tm, d), lambda i: (i, 0))
    vec = pl.BlockSpec((1, d), lambda i: (0, 0))
    return pl.pallas_call(
        functools.partial(_oproj_kernel, alpha=alpha),
        grid=(t // tm,),
        in_specs=[row, row, pl.BlockSpec((d, d), lambda i: (0, 0)), vec, vec],
        out_specs=[row, row, pl.BlockSpec((d, tm), lambda i: (0, i))],
        out_shape=[jax.ShapeDtypeStruct((t, d), F32), jax.ShapeDtypeStruct((t, d), BF16),
                   jax.ShapeDtypeStruct((d, t), BF16)],
        compiler_params=_params("parallel"),
        name="out_proj_ln1",
    )(m, x, wo, g, b)


def _top_k_rows(ss, k, tie_break):
    n = ss[0].shape[0]
    idx = lax.broadcasted_iota(jnp.int32, ss[0].shape, 0).astype(F32)
    ranks = [jnp.full(s.shape, float(k), F32) for s in ss]
    works = list(ss)
    tops = [[] for _ in ss]
    for r in range(k):
        for i in range(len(ss)):
            m = jnp.max(works[i], axis=0, keepdims=True)
            sel = works[i] == m
            if tie_break:
                first = jnp.min(jnp.where(sel, idx, float(n)), axis=0, keepdims=True)
                sel = idx == first
            ranks[i] = jnp.where(sel, float(r), ranks[i])
            works[i] = jnp.where(sel, -jnp.inf, works[i])
            tops[i].append(m)
    exact = None
    for rank in ranks:
        picked = jnp.sum((rank < float(k)).astype(F32), axis=0, keepdims=True)
        ok = jnp.all(picked == float(k))
        exact = ok if exact is None else jnp.logical_and(exact, ok)
    return tuple(ranks), tuple(jnp.concatenate(t, axis=0) for t in tops), exact


def _merge_top_k(t1, t2, k):
    aidx = lax.broadcasted_iota(jnp.int32, t1.shape, 0).astype(F32)
    nb = jnp.zeros(t1.shape, F32)
    head = t1 + t2[0:1]
    top = head[0:1]
    z = jnp.zeros_like(top)
    for _ in range(k):
        m = jnp.max(head, axis=0, keepdims=True)
        first = jnp.min(jnp.where(head == m, aidx, float(k)), axis=0, keepdims=True)
        sel = aidx == first
        z = z + jnp.exp(m - top)
        nb = jnp.where(sel, nb + 1.0, nb)
        nxt = jnp.sum(jnp.where(sel, nb, 0.0), axis=0, keepdims=True)
        t2n = jnp.sum(jnp.where(aidx == nxt, t2, 0.0), axis=0, keepdims=True)
        t2n = jnp.where(nxt >= float(k), -jnp.inf, t2n)
        head = jnp.where(sel, t1 + t2n, head)
    return nb, z


def _route_kernel(x_ref, wq_ref, k1_ref, k2_ref, r2_ref, b2_ref, nb1_ref, a1_ref, q_ref):
    tc = x_ref.shape[0]
    half = k1_ref.shape[-1]
    k = PEER_TOPK
    h = pl.program_id(1)

    @pl.when(h == 0)
    def _():
        q_ref[...] = _dot(x_ref[...], wq_ref[...]).astype(BF16)

    k1 = k1_ref[0, 0].astype(BF16)
    k2 = k2_ref[0, 0].astype(BF16)
    nt = (((1,), (1,)), ((), ()))
    hs = pl.ds(pl.multiple_of(h * 2 * half, 2 * half), 2 * half)

    def chunk(c, tie_break):
        start = c * ROUTE_CHUNK
        ts = pl.ds(start if isinstance(c, int) else pl.multiple_of(start, ROUTE_CHUNK), ROUTE_CHUNK)
        qc = q_ref[ts, hs]
        s1 = lax.dot_general(k1, qc[:, :half], nt, preferred_element_type=F32)
        s2 = lax.dot_general(k2, qc[:, half:], nt, preferred_element_type=F32)
        (rank1, rank2), (t1, t2), exact = _top_k_rows((s1, s2), k, tie_break)
        nb, z = _merge_top_k(t1, t2, k)
        rank1 = rank1.astype(BF16)
        nb = nb.astype(BF16)
        nb1 = jnp.zeros(rank1.shape, BF16)
        for a in range(k):
            nb1 = jnp.where(rank1 == a, nb[a:a + 1], nb1)
        r2_ref[0, :, ts] = rank2.astype(BF16)
        b2_ref[0, :, ts] = jnp.exp(s2 - t2[0:1]).astype(BF16)
        nb1_ref[0, :, ts] = nb1.astype(F32)
        a1_ref[0, :, ts] = jnp.exp(s1 - t1[0:1]) * (0.5 / z)
        return exact

    exact = None
    for c in range(tc // ROUTE_CHUNK):
        ok = chunk(c, False)
        exact = ok if exact is None else jnp.logical_and(exact, ok)

    @pl.when(jnp.logical_not(exact))
    def _():
        def redo(c, carry):
            chunk(c, True)
            return carry
        lax.fori_loop(0, tc // ROUTE_CHUNK, redo, 0)


def _route(x1b, wq, keys, *, tc):
    t, d = x1b.shape
    heads, _, nkeys, half = keys.shape
    out = pl.BlockSpec((1, nkeys, tc), lambda i, h: (h, 0, i))
    shape = jax.ShapeDtypeStruct((heads, nkeys, t), F32)
    shape_bf16 = jax.ShapeDtypeStruct((heads, nkeys, t), BF16)
    return pl.pallas_call(
        _route_kernel,
        grid=(t // tc, heads),
        in_specs=[pl.BlockSpec((tc, d), lambda i, h: (i, 0)),
                  pl.BlockSpec(wq.shape, lambda i, h: (0, 0)),
                  pl.BlockSpec((1, 1, nkeys, half), lambda i, h: (h, 0, 0, 0)),
                  pl.BlockSpec((1, 1, nkeys, half), lambda i, h: (h, 1, 0, 0))],
        out_specs=[out, out, out, out],
        out_shape=[shape_bf16, shape_bf16, shape, shape],
        scratch_shapes=[pltpu.VMEM((tc, wq.shape[1]), BF16)],
        compiler_params=_params("parallel", "arbitrary"),
        name="peer_route",
    )(x1b, wq, keys, keys)


def _experts_kernel(xt_ref, u_ref, vt_ref, r2_ref, b2_ref, nb1_ref, a1_ref, y_ref, acc_ref, *, rows_per_step):
    s = pl.program_id(1)
    heads = r2_ref.shape[0]

    @pl.when(s == 0)
    def _():
        acc_ref[...] = jnp.zeros_like(acc_ref)

    nkeys = r2_ref.shape[1]
    sub = EXPERT_SUB_ROWS
    acts = []
    for q in range(rows_per_step // sub):
        wts = []
        for r in range(q * sub, (q + 1) * sub):
            i1 = s * rows_per_step + r
            wt = None
            for hd in range(heads):
                nb_row = nb1_ref[hd, pl.ds(i1, 1), :].astype(BF16)
                a_row = a1_ref[hd, pl.ds(i1, 1), :].astype(BF16)
                zero = jnp.zeros((), BF16)
                term = jnp.where(r2_ref[hd] < nb_row, b2_ref[hd], zero) * a_row
                wt = term if wt is None else wt + term
            wts.append(wt)
        w = jnp.concatenate(wts, axis=0)
        ht = _dot(u_ref[q * sub * nkeys:(q + 1) * sub * nkeys, :], xt_ref[...])
        two_gelu = ht * (1.0 + lax.erf(ht * SQRT_HALF))
        acts.append(w * two_gelu.astype(BF16))
    act = jnp.concatenate(acts, axis=0)
    acc_ref[...] += _dot(vt_ref[...], act)

    @pl.when(s == pl.num_programs(1) - 1)
    def _():
        y_ref[...] = acc_ref[...].T


def _experts(x1t, ub, vt, r2, b2, nb1, a1, *, tm, rows_per_step):
    d, t = x1t.shape
    heads, nkeys, _ = r2.shape
    eb = rows_per_step * nkeys
    route = pl.BlockSpec((heads, nkeys, tm), lambda i, s: (0, 0, i))
    return pl.pallas_call(
        functools.partial(_experts_kernel, rows_per_step=rows_per_step),
        grid=(t // tm, nkeys // rows_per_step),
        in_specs=[pl.BlockSpec((d, tm), lambda i, s: (0, i)),
                  pl.BlockSpec((eb, d), lambda i, s: (s, 0)),
                  pl.BlockSpec((d, eb), lambda i, s: (0, s)),
                  route, route, route, route],
        out_specs=pl.BlockSpec((tm, d), lambda i, s: (i, 0)),
        out_shape=jax.ShapeDtypeStruct((t, d), F32),
        scratch_shapes=[pltpu.VMEM((d, tm), F32)],
        compiler_params=_params("parallel", "arbitrary"),
        name="peer_experts",
    )(x1t, ub, vt, r2, b2, nb1, a1)


def _transpose_cast_kernel(v_ref, vt_ref):
    vt_ref[...] = v_ref[...].T.astype(BF16)


def _transpose_cast(v, *, rows):
    n, d = v.shape
    return pl.pallas_call(
        _transpose_cast_kernel,
        grid=(n // rows,),
        in_specs=[pl.BlockSpec((rows, d), lambda i: (i, 0))],
        out_specs=pl.BlockSpec((d, rows), lambda i: (0, i)),
        out_shape=jax.ShapeDtypeStruct((d, n), BF16),
        compiler_params=_params("parallel"),
        name="expert_v_layout",
    )(v)


def _tail_kernel(x1_ref, y_ref, p_ref, g_ref, b_ref, wg_ref, wp_ref, o_ref, *, alpha):
    x2 = _layer_norm(alpha * x1_ref[...] + y_ref[...], g_ref[...], b_ref[...])
    gate = _sigmoid(_dot(x2.astype(BF16), wg_ref[...]))
    proj = _dot(p_ref[...].astype(BF16), wp_ref[...])
    o_ref[...] = x2 + gate * proj


def _tail(x1, y, p, g, b, wg, wp, *, alpha, tm):
    t, d = x1.shape
    pd = p.shape[1]
    row = pl.BlockSpec((tm, d), lambda i: (i, 0))
    vec = pl.BlockSpec((1, d), lambda i: (0, 0))
    return pl.pallas_call(
        functools.partial(_tail_kernel, alpha=alpha),
        grid=(t // tm,),
        in_specs=[row, row, pl.BlockSpec((tm, pd), lambda i: (i, 0)), vec, vec,
                  pl.BlockSpec((d, d), lambda i: (0, 0)), pl.BlockSpec((pd, d), lambda i: (0, 0))],
        out_specs=row,
        out_shape=jax.ShapeDtypeStruct((t, d), F32),
        compiler_params=_params("parallel"),
        name="ln2_ple",
    )(x1, y, p, g, b, wg, wp)


def kernel(x, p, w_in, b_in, gm_ln_g, gm_ln_b, gm_ws, gm_bs, w_gm_out, cv_w, cv_b, cv_ln_g, cv_ln_b,
           w_cv_out, w_o, ln1_g, ln1_b, peer_wq, peer_keys, peer_u, peer_v, ln2_g, ln2_b,
           ple_w_gate, ple_w_proj):
    bsz, seq, d = x.shape
    depth = w_in.shape[0]
    t = bsz * seq
    alpha = (2.0 * depth) ** 0.25
    groups, blk, _ = gm_ws.shape[1:]
    gdim = d // groups
    tm = min(512, seq)

    xf = x.reshape(t, d)
    for i in range(depth):
        row = lambda a: a[i].reshape(1, -1)
        bs_b = jnp.broadcast_to(gm_bs[i][:, :, None], (groups, blk, gdim))
        sg, glu, ga, gb = _inproj(xf, w_in[i].astype(BF16), row(b_in), row(gm_ln_g), row(gm_ln_b),
                                  gm_ws[i], bs_b, tm=min(1024, t), tn=gdim)
        m = _mix(sg, glu, cv_w[i], row(cv_b), row(cv_ln_g), row(cv_ln_b),
                 w_gm_out[i].astype(BF16), w_cv_out[i].astype(BF16), ga, gb, seq=seq, tm=tm, tn=512)
        x1, x1b, x1t = _oproj(m, xf, w_o[i].astype(BF16), row(ln1_g), row(ln1_b), alpha=alpha, tm=tm)
        r2, b2, nb1, a1 = _route(x1b, peer_wq[i].astype(BF16), peer_keys[i], tc=min(1024, t))
        y = _experts(x1t, peer_u[i].astype(BF16), _transpose_cast(peer_v[i], rows=512), r2, b2, nb1, a1,
                     tm=tm, rows_per_step=8)
        xf = _tail(x1, y, p[i].reshape(t, -1), row(ln2_g), row(ln2_b),
                   ple_w_gate[i].astype(BF16), ple_w_proj[i].astype(BF16), alpha=alpha, tm=tm)
    return xf.reshape(bsz, seq, d)
```

```python
import functools

import jax
import jax.numpy as jnp
from jax import lax
from jax.experimental import pallas as pl
from jax.experimental.pallas import tpu as pltpu

LN_EPS = 1e-5
CHUNK = 64
PEER_TOPK = 16
SQRT_HALF = 0.7071067811865476
LANES = 128
SUBLANES = 8
EXPERT_SUB_ROWS = 2
ROUTE_CHUNK = 256
HALO = 32
VMEM_LIMIT = 56 * 1024 * 1024

F32 = jnp.float32
BF16 = jnp.bfloat16


def _gelu(x):
    return 0.5 * x * (1.0 + lax.erf(x * SQRT_HALF))


def _sigmoid(x):
    return 1.0 / (1.0 + jnp.exp(-x))


def _layer_norm(x, g, b):
    mu = jnp.mean(x, axis=-1, keepdims=True)
    d = x - mu
    var = jnp.mean(d * d, axis=-1, keepdims=True)
    return d * lax.rsqrt(var + LN_EPS) * g + b


def _dot(a, b):
    return jnp.dot(a, b, preferred_element_type=F32)


def _params(*sem):
    return pltpu.CompilerParams(dimension_semantics=sem, vmem_limit_bytes=VMEM_LIMIT)


def _inproj_kernel(x_ref, wu_ref, wv_ref, wa_ref, wb_ref, wga_ref, wgb_ref,
                   bu_ref, bv_ref, ba_ref, bb_ref, bga_ref, bgb_ref,
                   lng_ref, lnb_ref, ws_ref, bs_ref,
                   sg_ref, glu_ref, ga_ref, gb_ref, xb_ref):
    @pl.when(pl.program_id(1) == 0)
    def _():
        xb_ref[...] = x_ref[...].astype(BF16)

    x = xb_ref[...]
    tm = x.shape[0]
    blk = ws_ref.shape[1]

    def proj(w_ref, b_ref):
        return _dot(x, w_ref[...]) + b_ref[...]

    u = _gelu(proj(wu_ref, bu_ref))
    v = _gelu(proj(wv_ref, bv_ref))
    vn = _layer_norm(v, lng_ref[...], lnb_ref[...]).astype(BF16)
    row = lax.broadcasted_iota(jnp.int32, (blk, blk), 0)
    col = lax.broadcasted_iota(jnp.int32, (blk, blk), 1)
    w = jnp.where(col // CHUNK <= row // CHUNK, ws_ref[0], 0.0).astype(BF16)
    bs = bs_ref[0]
    for nb in range(tm // blk):
        rows = slice(nb * blk, (nb + 1) * blk)
        mixed = _dot(w, vn[rows]) + bs
        sg_ref[rows, :] = (u[rows] * mixed).astype(BF16)

    glu_ref[...] = proj(wa_ref, ba_ref) * _sigmoid(proj(wb_ref, bb_ref))
    ga_ref[...] = _sigmoid(proj(wga_ref, bga_ref))
    gb_ref[...] = _sigmoid(proj(wgb_ref, bgb_ref))


def _inproj(x, w_in, b_in, ln_g, ln_b, ws, bs_b, *, tm, tn):
    t, d = x.shape
    nseg = w_in.shape[1] // d
    ncol = d // tn
    w_specs = [pl.BlockSpec((d, tn), functools.partial(lambda i, j, s: (0, s * ncol + j), s=s))
               for s in range(nseg)]
    b_specs = [pl.BlockSpec((1, tn), functools.partial(lambda i, j, s: (0, s * ncol + j), s=s))
               for s in range(nseg)]
    vec = pl.BlockSpec((1, tn), lambda i, j: (0, j))
    tile = pl.BlockSpec((tm, tn), lambda i, j: (i, j))
    return pl.pallas_call(
        _inproj_kernel,
        grid=(t // tm, ncol),
        in_specs=[pl.BlockSpec((tm, d), lambda i, j: (i, 0))] + w_specs + b_specs + [
            vec, vec,
            pl.BlockSpec((1,) + ws.shape[1:], lambda i, j: (j, 0, 0)),
            pl.BlockSpec((1,) + bs_b.shape[1:], lambda i, j: (j, 0, 0)),
        ],
        out_specs=[tile, tile, tile, tile],
        out_shape=[jax.ShapeDtypeStruct((t, d), BF16), jax.ShapeDtypeStruct((t, d), F32),
                   jax.ShapeDtypeStruct((t, d), F32), jax.ShapeDtypeStruct((t, d), F32)],
        scratch_shapes=[pltpu.VMEM((tm, d), BF16)],
        compiler_params=_params("parallel", "arbitrary"),
        name="inproj",
    )(x, *([w_in] * nseg), *([b_in] * nseg), ln_g, ln_b, ws, bs_b)


def _mix_kernel(sg_ref, glu_ref, halo_ref, cw_ref, cb_ref, clg_ref, clb_ref,
                wa_ref, wb_ref, ga_ref, gb_ref, m_ref, buf_ref, c_ref, cn_ref, *, blocks_per_seq):
    i = pl.program_id(0)
    j = pl.program_id(1)
    tm, d = glu_ref.shape
    taps = cw_ref.shape[0]

    @pl.when(j == 0)
    def _():
        keep = (i % blocks_per_seq != 0).astype(F32)
        buf_ref[0:HALO, :] = halo_ref[...] * keep
        buf_ref[HALO:HALO + tm, :] = glu_ref[...]
        buf_ref[HALO + tm:, :] = jnp.zeros((SUBLANES, d), F32)
        base = HALO - (taps - 1)

        def cols(c, carry):
            cs = pl.ds(pl.multiple_of(c * LANES, LANES), LANES)
            acc = jnp.broadcast_to(cb_ref[:, cs], (tm, LANES))
            for sh in range(SUBLANES):
                part = None
                for k in range(taps):
                    off = base + k
                    if off % SUBLANES == sh:
                        term = cw_ref[k:k + 1, cs] * buf_ref[off - sh:off - sh + tm + SUBLANES, cs]
                        part = term if part is None else part + term
                if part is not None:
                    acc = acc + part[sh:sh + tm]
            c_ref[:, cs] = acc
            return carry

        lax.fori_loop(0, d // LANES, cols, 0)
        c = _layer_norm(c_ref[...], clg_ref[...], clb_ref[...])
        cn_ref[...] = (c * _sigmoid(c)).astype(BF16)

    ya = _dot(sg_ref[...], wa_ref[...])
    yb = _dot(cn_ref[...], wb_ref[...])
    m_ref[...] = (ga_ref[...] * ya + gb_ref[...] * yb).astype(BF16)


def _mix(sg, glu, cv_w, cv_b, cv_ln_g, cv_ln_b, wa, wb, ga, gb, *, seq, tm, tn):
    t, d = glu.shape
    assert seq % tm == 0 and tm % HALO == 0 and cv_w.shape[0] - 1 <= HALO
    hb = tm // HALO
    row = pl.BlockSpec((tm, d), lambda i, j: (i, 0))
    vec = pl.BlockSpec((1, d), lambda i, j: (0, 0))
    wcol = pl.BlockSpec((d, tn), lambda i, j: (0, j))
    tile = pl.BlockSpec((tm, tn), lambda i, j: (i, j))
    return pl.pallas_call(
        functools.partial(_mix_kernel, blocks_per_seq=seq // tm),
        grid=(t // tm, d // tn),
        in_specs=[row, row,
                  pl.BlockSpec((HALO, d), lambda i, j: (jnp.maximum(i * hb - 1, 0), 0)),
                  pl.BlockSpec(cv_w.shape, lambda i, j: (0, 0)),
                  vec, vec, vec, wcol, wcol, tile, tile],
        out_specs=tile,
        out_shape=jax.ShapeDtypeStruct((t, d), BF16),
        scratch_shapes=[pltpu.VMEM((HALO + tm + SUBLANES, d), F32), pltpu.VMEM((tm, d), F32),
                        pltpu.VMEM((tm, d), BF16)],
        compiler_params=_params("parallel", "arbitrary"),
        name="branch_mix",
    )(sg, glu, glu, cv_w, cv_b, cv_ln_g, cv_ln_b, wa, wb, ga, gb)


def _oproj_kernel(m_ref, x_ref, wo_ref, g_ref, b_ref, x1_ref, x1b_ref, x1t_ref, *, alpha):
    mix = _dot(m_ref[...], wo_ref[...])
    x1 = _layer_norm(alpha * x_ref[...] + mix, g_ref[...], b_ref[...])
    x1_ref[...] = x1
    x1b_ref[...] = x1.astype(BF16)
    x1t_ref[...] = x1.T.astype(BF16)


def _oproj(m, x, wo, g, b, *, alpha, tm):
    t, d = x.shape
    row = pl.BlockSpec((tm, d), lambda i: (i, 0))
    vec = pl.BlockSpec((1, d), lambda i: (0, 0))
    return pl.pallas_call(
        functools.partial(_oproj_kernel, alpha=alpha),
        grid=(t // tm,),
        in_specs=[row, row, pl.BlockSpec((d, d), lambda i: (0, 0)), vec, vec],
        out_specs=[row, row, pl.BlockSpec((d, tm), lambda i: (0, i))],
        out_shape=[jax.ShapeDtypeStruct((t, d), F32), jax.ShapeDtypeStruct((t, d), BF16),
                   jax.ShapeDtypeStruct((d, t), BF16)],
        compiler_params=_params("parallel"),
        name="out_proj_ln1",
    )(m, x, wo, g, b)


def _top_k_rows(ss, k, tie_break):
    n = ss[0].shape[0]
    idx = lax.broadcasted_iota(jnp.int32, ss[0].shape, 0).astype(F32)
    ranks = [jnp.full(s.shape, float(k), F32) for s in ss]
    works = list(ss)
    tops = [[] for _ in ss]
    for r in range(k):
        for i in range(len(ss)):
            m = jnp.max(works[i], axis=0, keepdims=True)
            sel = works[i] == m
            if tie_break:
                first = jnp.min(jnp.where(sel, idx, float(n)), axis=0, keepdims=True)
                sel = idx == first
            ranks[i] = jnp.where(sel, float(r), ranks[i])
            works[i] = jnp.where(sel, -jnp.inf, works[i])
            tops[i].append(m)
    exact = None
    for rank in ranks:
        picked = jnp.sum((rank < float(k)).astype(F32), axis=0, keepdims=True)
        ok = jnp.all(picked == float(k))
        exact = ok if exact is None else jnp.logical_and(exact, ok)
    return tuple(ranks), tuple(jnp.concatenate(t, axis=0) for t in tops), exact


def _merge_top_k(t1, t2, k):
    aidx = lax.broadcasted_iota(jnp.int32, t1.shape, 0).astype(F32)
    nb = jnp.zeros(t1.shape, F32)
    head = t1 + t2[0:1]
    top = head[0:1]
    z = jnp.zeros_like(top)
    for _ in range(k):
        m = jnp.max(head, axis=0, keepdims=True)
        first = jnp.min(jnp.where(head == m, aidx, float(k)), axis=0, keepdims=True)
        sel = aidx == first
        z = z + jnp.exp(m - top)
        nb = jnp.where(sel, nb + 1.0, nb)
        nxt = jnp.sum(jnp.where(sel, nb, 0.0), axis=0, keepdims=True)
        t2n = jnp.sum(jnp.where(aidx == nxt, t2, 0.0), axis=0, keepdims=True)
        t2n = jnp.where(nxt >= float(k), -jnp.inf, t2n)
        head = jnp.where(sel, t1 + t2n, head)
    return nb, z


def _route_kernel(x_ref, wq0_ref, wqn_ref, k1_ref, k2_ref, u_ref, v_ref,
                  r2_ref, b2_ref, nb1_ref, a1_ref, ub_ref, vt_ref, q_ref):
    ub_ref[...] = u_ref[...].astype(BF16)
    vt_ref[...] = v_ref[...].T.astype(BF16)
    tc = x_ref.shape[0]
    half = k1_ref.shape[-1]
    k = PEER_TOPK
    h = pl.program_id(1)
    slot = h % 2

    @pl.when(h == 0)
    def _():
        q_ref[0] = _dot(x_ref[...], wq0_ref[...]).astype(BF16)

    k1 = k1_ref[0, 0].astype(BF16)
    k2 = k2_ref[0, 0].astype(BF16)
    nt = (((1,), (1,)), ((), ()))

    def chunk(c, tie_break):
        start = c * ROUTE_CHUNK
        ts = pl.ds(start if isinstance(c, int) else pl.multiple_of(start, ROUTE_CHUNK), ROUTE_CHUNK)
        qc = q_ref[slot, ts, :]
        s1 = lax.dot_general(k1, qc[:, :half], nt, preferred_element_type=F32)
        s2 = lax.dot_general(k2, qc[:, half:], nt, preferred_element_type=F32)
        (rank1, rank2), (t1, t2), exact = _top_k_rows((s1, s2), k, tie_break)
        nb, z = _merge_top_k(t1, t2, k)
        rank1 = rank1.astype(BF16)
        nb = nb.astype(BF16)
        nb1 = jnp.zeros(rank1.shape, BF16)
        for a in range(k):
            nb1 = jnp.where(rank1 == a, nb[a:a + 1], nb1)
        r2_ref[0, :, ts] = rank2.astype(BF16)
        b2_ref[0, :, ts] = jnp.exp(s2 - t2[0:1]).astype(BF16)
        nb1_ref[0, :, ts] = nb1.astype(F32)
        a1_ref[0, :, ts] = jnp.exp(s1 - t1[0:1]) * (0.5 / z)
        return exact

    exact = None
    for c in range(tc // ROUTE_CHUNK):
        ok = chunk(c, False)
        exact = ok if exact is None else jnp.logical_and(exact, ok)
    q_ref[1 - slot] = _dot(x_ref[...], wqn_ref[...]).astype(BF16)

    @pl.when(jnp.logical_not(exact))
    def _():
        def redo(c, carry):
            chunk(c, True)
            return carry
        lax.fori_loop(0, tc // ROUTE_CHUNK, redo, 0)


def _route(x1b, wq, keys, u, v, *, tc):
    t, d = x1b.shape
    heads, _, nkeys, half = keys.shape
    nexp = u.shape[0]
    steps = (t // tc) * heads
    assert nexp % steps == 0 and (nexp // steps) % LANES == 0
    er = nexp // steps
    out = pl.BlockSpec((1, nkeys, tc), lambda i, h: (h, 0, i))
    shape = jax.ShapeDtypeStruct((heads, nkeys, t), F32)
    shape_bf16 = jax.ShapeDtypeStruct((heads, nkeys, t), BF16)
    rows = pl.BlockSpec((er, d), lambda i, h: (i * heads + h, 0))
    return pl.pallas_call(
        _route_kernel,
        grid=(t // tc, heads),
        in_specs=[pl.BlockSpec((tc, d), lambda i, h: (i, 0)),
                  pl.BlockSpec((d, 2 * half), lambda i, h: (0, 0)),
                  pl.BlockSpec((d, 2 * half), lambda i, h: (0, jnp.minimum(h + 1, heads - 1))),
                  pl.BlockSpec((1, 1, nkeys, half), lambda i, h: (h, 0, 0, 0)),
                  pl.BlockSpec((1, 1, nkeys, half), lambda i, h: (h, 1, 0, 0)),
                  rows, rows],
        out_specs=[out, out, out, out, rows, pl.BlockSpec((d, er), lambda i, h: (0, i * heads + h))],
        out_shape=[shape_bf16, shape_bf16, shape, shape,
                   jax.ShapeDtypeStruct((nexp, d), BF16), jax.ShapeDtypeStruct((d, nexp), BF16)],
        scratch_shapes=[pltpu.VMEM((2, tc, 2 * half), BF16)],
        compiler_params=_params("parallel", "arbitrary"),
        name="peer_route",
    )(x1b, wq, wq, keys, keys, u, v)


def _experts_kernel(xt_ref, u_ref, vt_ref, r2_ref, b2_ref, nb1_ref, a1_ref, y_ref, acc_ref, *, rows_per_step):
    s = pl.program_id(1)
    heads = r2_ref.shape[0]

    @pl.when(s == 0)
    def _():
        acc_ref[...] = jnp.zeros_like(acc_ref)

    nkeys = r2_ref.shape[1]
    sub = EXPERT_SUB_ROWS
    acts = []
    for q in range(rows_per_step // sub):
        wts = []
        for r in range(q * sub, (q + 1) * sub):
            i1 = s * rows_per_step + r
            wt = None
            for hd in range(heads):
                nb_row = nb1_ref[hd, pl.ds(i1, 1), :].astype(BF16)
                a_row = a1_ref[hd, pl.ds(i1, 1), :].astype(BF16)
                zero = jnp.zeros((), BF16)
                term = jnp.where(r2_ref[hd] < nb_row, b2_ref[hd], zero) * a_row
                wt = term if wt is None else wt + term
            wts.append(wt)
        w = jnp.concatenate(wts, axis=0)
        ht = _dot(u_ref[q * sub * nkeys:(q + 1) * sub * nkeys, :], xt_ref[...])
        two_gelu = ht * (1.0 + lax.erf(ht * SQRT_HALF))
        acts.append(w * two_gelu.astype(BF16))
    act = jnp.concatenate(acts, axis=0)
    acc_ref[...] += _dot(vt_ref[...], act)

    @pl.when(s == pl.num_programs(1) - 1)
    def _():
        y_ref[...] = acc_ref[...].T


def _experts(x1t, ub, vt, r2, b2, nb1, a1, *, tm, rows_per_step):
    d, t = x1t.shape
    heads, nkeys, _ = r2.shape
    eb = rows_per_step * nkeys
    route = pl.BlockSpec((heads, nkeys, tm), lambda i, s: (0, 0, i))
    return pl.pallas_call(
        functools.partial(_experts_kernel, rows_per_step=rows_per_step),
        grid=(t // tm, nkeys // rows_per_step),
        in_specs=[pl.BlockSpec((d, tm), lambda i, s: (0, i)),
                  pl.BlockSpec((eb, d), lambda i, s: (s, 0)),
                  pl.BlockSpec((d, eb), lambda i, s: (0, s)),
                  route, route, route, route],
        out_specs=pl.BlockSpec((tm, d), lambda i, s: (i, 0)),
        out_shape=jax.ShapeDtypeStruct((t, d), F32),
        scratch_shapes=[pltpu.VMEM((d, tm), F32)],
        compiler_params=_params("parallel", "arbitrary"),
        name="peer_experts",
    )(x1t, ub, vt, r2, b2, nb1, a1)


def _tail_kernel(x1_ref, y_ref, p_ref, g_ref, b_ref, wg_ref, wp_ref, o_ref, *, alpha):
    x2 = _layer_norm(alpha * x1_ref[...] + y_ref[...], g_ref[...], b_ref[...])
    gate = _sigmoid(_dot(x2.astype(BF16), wg_ref[...]))
    proj = _dot(p_ref[...].astype(BF16), wp_ref[...])
    o_ref[...] = x2 + gate * proj


def _tail(x1, y, p, g, b, wg, wp, *, alpha, tm):
    t, d = x1.shape
    pd = p.shape[1]
    row = pl.BlockSpec((tm, d), lambda i: (i, 0))
    vec = pl.BlockSpec((1, d), lambda i: (0, 0))
    return pl.pallas_call(
        functools.partial(_tail_kernel, alpha=alpha),
        grid=(t // tm,),
        in_specs=[row, row, pl.BlockSpec((tm, pd), lambda i: (i, 0)), vec, vec,
                  pl.BlockSpec((d, d), lambda i: (0, 0)), pl.BlockSpec((pd, d), lambda i: (0, 0))],
        out_specs=row,
        out_shape=jax.ShapeDtypeStruct((t, d), F32),
        compiler_params=_params("parallel"),
        name="ln2_ple",
    )(x1, y, p, g, b, wg, wp)


def kernel(x, p, w_in, b_in, gm_ln_g, gm_ln_b, gm_ws, gm_bs, w_gm_out, cv_w, cv_b, cv_ln_g, cv_ln_b,
           w_cv_out, w_o, ln1_g, ln1_b, peer_wq, peer_keys, peer_u, peer_v, ln2_g, ln2_b,
           ple_w_gate, ple_w_proj):
    bsz, seq, d = x.shape
    depth = w_in.shape[0]
    t = bsz * seq
    alpha = (2.0 * depth) ** 0.25
    groups, blk, _ = gm_ws.shape[1:]
    gdim = d // groups
    tm = min(512, seq)

    xf = x.reshape(t, d)
    for i in range(depth):
        row = lambda a: a[i].reshape(1, -1)
        bs_b = jnp.broadcast_to(gm_bs[i][:, :, None], (groups, blk, gdim))
        sg, glu, ga, gb = _inproj(xf, w_in[i].astype(BF16), row(b_in), row(gm_ln_g), row(gm_ln_b),
                                  gm_ws[i], bs_b, tm=min(1024, t), tn=gdim)
        m = _mix(sg, glu, cv_w[i], row(cv_b), row(cv_ln_g), row(cv_ln_b),
                 w_gm_out[i].astype(BF16), w_cv_out[i].astype(BF16), ga, gb, seq=seq, tm=tm, tn=512)
        x1, x1b, x1t = _oproj(m, xf, w_o[i].astype(BF16), row(ln1_g), row(ln1_b), alpha=alpha, tm=tm)
        r2, b2, nb1, a1, ub, vt = _route(x1b, peer_wq[i].astype(BF16), peer_keys[i], peer_u[i], peer_v[i],
                                         tc=min(1024, t))
        y = _experts(x1t, ub, vt, r2, b2, nb1, a1, tm=tm, rows_per_step=8)
        xf = _tail(x1, y, p[i].reshape(t, -1), row(ln2_g), row(ln2_b),
                   ple_w_gate[i].astype(BF16), ple_w_proj[i].astype(BF16), alpha=alpha, tm=tm)
    return xf.reshape(bsz, seq, d)
```

```python
import functools

import jax
import jax.numpy as jnp
from jax import lax
from jax.experimental import pallas as pl
from jax.experimental.pallas import tpu as pltpu

LN_EPS = 1e-5
CHUNK = 64
PEER_TOPK = 16
SQRT_HALF = 0.7071067811865476
LANES = 128
SUBLANES = 8
EXPERT_SUB_ROWS = 2
ROUTE_CHUNK = 256
HALO = 32
VMEM_LIMIT = 56 * 1024 * 1024

F32 = jnp.float32
BF16 = jnp.bfloat16


def _gelu(x):
    return 0.5 * x * (1.0 + lax.erf(x * SQRT_HALF))


def _sigmoid(x):
    return 1.0 / (1.0 + jnp.exp(-x))


def _layer_norm(x, g, b):
    mu = jnp.mean(x, axis=-1, keepdims=True)
    d = x - mu
    var = jnp.mean(d * d, axis=-1, keepdims=True)
    return d * lax.rsqrt(var + LN_EPS) * g + b


def _dot(a, b):
    return jnp.dot(a, b, preferred_element_type=F32)


def _params(*sem):
    return pltpu.CompilerParams(dimension_semantics=sem, vmem_limit_bytes=VMEM_LIMIT)


N_INPROJ_INPUTS = 17


def _inproj_kernel(*refs, n_cast):
    (x_ref, wu_ref, wv_ref, wa_ref, wb_ref, wga_ref, wgb_ref,
     bu_ref, bv_ref, ba_ref, bb_ref, bga_ref, bgb_ref,
     lng_ref, lnb_ref, ws_ref, bs_ref) = refs[:N_INPROJ_INPUTS]
    cast_in = refs[N_INPROJ_INPUTS:N_INPROJ_INPUTS + n_cast]
    sg_ref, glu_ref, ga_ref, gb_ref = refs[N_INPROJ_INPUTS + n_cast:N_INPROJ_INPUTS + n_cast + 4]
    cast_out = refs[N_INPROJ_INPUTS + n_cast + 4:N_INPROJ_INPUTS + 2 * n_cast + 4]
    xb_ref = refs[-1]

    for src, dst in zip(cast_in, cast_out):
        dst[...] = src[...].astype(BF16)

    @pl.when(pl.program_id(1) == 0)
    def _():
        xb_ref[...] = x_ref[...].astype(BF16)

    x = xb_ref[...]
    tm = x.shape[0]
    blk = ws_ref.shape[1]

    def proj(w_ref, b_ref):
        return _dot(x, w_ref[...]) + b_ref[...]

    u = _gelu(proj(wu_ref, bu_ref))
    v = _gelu(proj(wv_ref, bv_ref))
    vn = _layer_norm(v, lng_ref[...], lnb_ref[...]).astype(BF16)
    row = lax.broadcasted_iota(jnp.int32, (blk, blk), 0)
    col = lax.broadcasted_iota(jnp.int32, (blk, blk), 1)
    w = jnp.where(col // CHUNK <= row // CHUNK, ws_ref[0], 0.0).astype(BF16)
    bs = bs_ref[0]
    for nb in range(tm // blk):
        rows = slice(nb * blk, (nb + 1) * blk)
        mixed = _dot(w, vn[rows]) + bs
        sg_ref[rows, :] = (u[rows] * mixed).astype(BF16)

    glu_ref[...] = proj(wa_ref, ba_ref) * _sigmoid(proj(wb_ref, bb_ref))
    ga_ref[...] = _sigmoid(proj(wga_ref, bga_ref))
    gb_ref[...] = _sigmoid(proj(wgb_ref, bgb_ref))


def _inproj(x, w_in, b_in, ln_g, ln_b, ws, bs_b, later_weights, *, tm, tn):
    t, d = x.shape
    nseg = w_in.shape[1] // d
    ncol = d // tn
    steps = (t // tm) * ncol
    cast_specs, cast_shapes = [], []
    for w in later_weights:
        assert w.shape[0] % steps == 0 and (w.shape[0] // steps) % (2 * SUBLANES) == 0
        cast_specs.append(pl.BlockSpec((w.shape[0] // steps, w.shape[1]), lambda i, j: (i * ncol + j, 0)))
        cast_shapes.append(jax.ShapeDtypeStruct(w.shape, BF16))
    w_specs = [pl.BlockSpec((d, tn), functools.partial(lambda i, j, s: (0, s * ncol + j), s=s))
               for s in range(nseg)]
    b_specs = [pl.BlockSpec((1, tn), functools.partial(lambda i, j, s: (0, s * ncol + j), s=s))
               for s in range(nseg)]
    vec = pl.BlockSpec((1, tn), lambda i, j: (0, j))
    tile = pl.BlockSpec((tm, tn), lambda i, j: (i, j))
    in_specs = [pl.BlockSpec((tm, d), lambda i, j: (i, 0))] + w_specs + b_specs + [
        vec, vec,
        pl.BlockSpec((1,) + ws.shape[1:], lambda i, j: (j, 0, 0)),
        pl.BlockSpec((1,) + bs_b.shape[1:], lambda i, j: (j, 0, 0)),
    ]
    assert len(in_specs) == N_INPROJ_INPUTS
    outs = pl.pallas_call(
        functools.partial(_inproj_kernel, n_cast=len(later_weights)),
        grid=(t // tm, ncol),
        in_specs=in_specs + cast_specs,
        out_specs=[tile, tile, tile, tile] + cast_specs,
        out_shape=[jax.ShapeDtypeStruct((t, d), BF16), jax.ShapeDtypeStruct((t, d), F32),
                   jax.ShapeDtypeStruct((t, d), F32), jax.ShapeDtypeStruct((t, d), F32)] + cast_shapes,
        scratch_shapes=[pltpu.VMEM((tm, d), BF16)],
        compiler_params=_params("parallel", "arbitrary"),
        name="inproj",
    )(x, *([w_in] * nseg), *([b_in] * nseg), ln_g, ln_b, ws, bs_b, *later_weights)
    return outs[:4], outs[4:]


def _mix_kernel(sg_ref, glu_ref, halo_ref, cw_ref, cb_ref, clg_ref, clb_ref,
                wa_ref, wb_ref, ga_ref, gb_ref, m_ref, buf_ref, c_ref, cn_ref, *, blocks_per_seq):
    i = pl.program_id(0)
    j = pl.program_id(1)
    tm, d = glu_ref.shape
    taps = cw_ref.shape[0]

    @pl.when(j == 0)
    def _():
        keep = (i % blocks_per_seq != 0).astype(F32)
        buf_ref[0:HALO, :] = halo_ref[...] * keep
        buf_ref[HALO:HALO + tm, :] = glu_ref[...]
        buf_ref[HALO + tm:, :] = jnp.zeros((SUBLANES, d), F32)
        base = HALO - (taps - 1)

        def cols(c, carry):
            cs = pl.ds(pl.multiple_of(c * LANES, LANES), LANES)
            acc = jnp.broadcast_to(cb_ref[:, cs], (tm, LANES))
            for sh in range(SUBLANES):
                part = None
                for k in range(taps):
                    off = base + k
                    if off % SUBLANES == sh:
                        term = cw_ref[k:k + 1, cs] * buf_ref[off - sh:off - sh + tm + SUBLANES, cs]
                        part = term if part is None else part + term
                if part is not None:
                    acc = acc + part[sh:sh + tm]
            c_ref[:, cs] = acc
            return carry

        lax.fori_loop(0, d // LANES, cols, 0)
        c = _layer_norm(c_ref[...], clg_ref[...], clb_ref[...])
        cn_ref[...] = (c * _sigmoid(c)).astype(BF16)

    ya = _dot(sg_ref[...], wa_ref[...])
    yb = _dot(cn_ref[...], wb_ref[...])
    m_ref[...] = (ga_ref[...] * ya + gb_ref[...] * yb).astype(BF16)


def _mix(sg, glu, cv_w, cv_b, cv_ln_g, cv_ln_b, wa, wb, ga, gb, *, seq, tm, tn):
    t, d = glu.shape
    assert seq % tm == 0 and tm % HALO == 0 and cv_w.shape[0] - 1 <= HALO
    hb = tm // HALO
    row = pl.BlockSpec((tm, d), lambda i, j: (i, 0))
    vec = pl.BlockSpec((1, d), lambda i, j: (0, 0))
    wcol = pl.BlockSpec((d, tn), lambda i, j: (0, j))
    tile = pl.BlockSpec((tm, tn), lambda i, j: (i, j))
    return pl.pallas_call(
        functools.partial(_mix_kernel, blocks_per_seq=seq // tm),
        grid=(t // tm, d // tn),
        in_specs=[row, row,
                  pl.BlockSpec((HALO, d), lambda i, j: (jnp.maximum(i * hb - 1, 0), 0)),
                  pl.BlockSpec(cv_w.shape, lambda i, j: (0, 0)),
                  vec, vec, vec, wcol, wcol, tile, tile],
        out_specs=tile,
        out_shape=jax.ShapeDtypeStruct((t, d), BF16),
        scratch_shapes=[pltpu.VMEM((HALO + tm + SUBLANES, d), F32), pltpu.VMEM((tm, d), F32),
                        pltpu.VMEM((tm, d), BF16)],
        compiler_params=_params("parallel", "arbitrary"),
        name="branch_mix",
    )(sg, glu, glu, cv_w, cv_b, cv_ln_g, cv_ln_b, wa, wb, ga, gb)


def _oproj_kernel(m_ref, x_ref, wo_ref, g_ref, b_ref, x1_ref, x1b_ref, x1t_ref, *, alpha):
    mix = _dot(m_ref[...], wo_ref[...])
    x1 = _layer_norm(alpha * x_ref[...] + mix, g_ref[...], b_ref[...])
    x1_ref[...] = x1
    x1b_ref[...] = x1.astype(BF16)
    x1t_ref[...] = x1.T.astype(BF16)


def _oproj(m, x, wo, g, b, *, alpha, tm):
    t, d = x.shape
    row = pl.BlockSpec((tm, d), lambda i: (i, 0))
    vec = pl.BlockSpec((1, d), lambda i: (0, 0))
    return pl.pallas_call(
        functools.partial(_oproj_kernel, alpha=alpha),
        grid=(t // tm,),
        in_specs=[row, row, pl.BlockSpec((d, d), lambda i: (0, 0)), vec, vec],
        out_specs=[row, row, pl.BlockSpec((d, tm), lambda i: (0, i))],
        out_shape=[jax.ShapeDtypeStruct((t, d), F32), jax.ShapeDtypeStruct((t, d), BF16),
                   jax.ShapeDtypeStruct((d, t), BF16)],
        compiler_params=_params("parallel"),
        name="out_proj_ln1",
    )(m, x, wo, g, b)


def _top_k_rows(ss, k, tie_break):
    n = ss[0].shape[0]
    idx = lax.broadcasted_iota(jnp.int32, ss[0].shape, 0).astype(F32)
    ranks = [jnp.full(s.shape, float(k), F32) for s in ss]
    works = list(ss)
    tops = [[] for _ in ss]
    for r in range(k):
        for i in range(len(ss)):
            m = jnp.max(works[i], axis=0, keepdims=True)
            sel = works[i] == m
            if tie_break:
                first = jnp.min(jnp.where(sel, idx, float(n)), axis=0, keepdims=True)
                sel = idx == first
            ranks[i] = jnp.where(sel, float(r), ranks[i])
            works[i] = jnp.where(sel, -jnp.inf, works[i])
            tops[i].append(m)
    exact = None
    for rank in ranks:
        picked = jnp.sum((rank < float(k)).astype(F32), axis=0, keepdims=True)
        ok = jnp.all(picked == float(k))
        exact = ok if exact is None else jnp.logical_and(exact, ok)
    return tuple(ranks), tuple(jnp.concatenate(t, axis=0) for t in tops), exact


def _merge_top_k(t1, t2, k):
    aidx = lax.broadcasted_iota(jnp.int32, t1.shape, 0).astype(F32)
    nb = jnp.zeros(t1.shape, F32)
    head = t1 + t2[0:1]
    top = head[0:1]
    z = jnp.zeros_like(top)
    for _ in range(k):
        m = jnp.max(head, axis=0, keepdims=True)
        first = jnp.min(jnp.where(head == m, aidx, float(k)), axis=0, keepdims=True)
        sel = aidx == first
        z = z + jnp.exp(m - top)
        nb = jnp.where(sel, nb + 1.0, nb)
        nxt = jnp.sum(jnp.where(sel, nb, 0.0), axis=0, keepdims=True)
        t2n = jnp.sum(jnp.where(aidx == nxt, t2, 0.0), axis=0, keepdims=True)
        t2n = jnp.where(nxt >= float(k), -jnp.inf, t2n)
        head = jnp.where(sel, t1 + t2n, head)
    return nb, z


def _route_kernel(x_ref, wq0_ref, wqn_ref, k1_ref, k2_ref, u_ref, v_ref,
                  r2_ref, b2_ref, nb1_ref, a1_ref, ub_ref, vt_ref, q_ref):
    ub_ref[...] = u_ref[...].astype(BF16)
    vt_ref[...] = v_ref[...].T.astype(BF16)
    tc = x_ref.shape[0]
    half = k1_ref.shape[-1]
    k = PEER_TOPK
    h = pl.program_id(1)
    slot = h % 2

    @pl.when(h == 0)
    def _():
        q_ref[0] = _dot(x_ref[...], wq0_ref[...]).astype(BF16)

    k1 = k1_ref[0, 0].astype(BF16)
    k2 = k2_ref[0, 0].astype(BF16)
    nt = (((1,), (1,)), ((), ()))

    def chunk(c, tie_break):
        start = c * ROUTE_CHUNK
        ts = pl.ds(start if isinstance(c, int) else pl.multiple_of(start, ROUTE_CHUNK), ROUTE_CHUNK)
        qc = q_ref[slot, ts, :]
        s1 = lax.dot_general(k1, qc[:, :half], nt, preferred_element_type=F32)
        s2 = lax.dot_general(k2, qc[:, half:], nt, preferred_element_type=F32)
        (rank1, rank2), (t1, t2), exact = _top_k_rows((s1, s2), k, tie_break)
        nb, z = _merge_top_k(t1, t2, k)
        rank1 = rank1.astype(BF16)
        nb = nb.astype(BF16)
        nb1 = jnp.zeros(rank1.shape, BF16)
        for a in range(k):
            nb1 = jnp.where(rank1 == a, nb[a:a + 1], nb1)
        r2_ref[0, :, ts] = rank2.astype(BF16)
        b2_ref[0, :, ts] = jnp.exp(s2 - t2[0:1]).astype(BF16)
        nb1_ref[0, :, ts] = nb1.astype(F32)
        a1_ref[0, :, ts] = jnp.exp(s1 - t1[0:1]) * (0.5 / z)
        return exact

    exact = None
    for c in range(tc // ROUTE_CHUNK):
        ok = chunk(c, False)
        exact = ok if exact is None else jnp.logical_and(exact, ok)
    q_ref[1 - slot] = _dot(x_ref[...], wqn_ref[...]).astype(BF16)

    @pl.when(jnp.logical_not(exact))
    def _():
        def redo(c, carry):
            chunk(c, True)
            return carry
        lax.fori_loop(0, tc // ROUTE_CHUNK, redo, 0)


def _route(x1b, wq, keys, u, v, *, tc):
    t, d = x1b.shape
    heads, _, nkeys, half = keys.shape
    nexp = u.shape[0]
    steps = (t // tc) * heads
    assert nexp % steps == 0 and (nexp // steps) % LANES == 0
    er = nexp // steps
    out = pl.BlockSpec((1, nkeys, tc), lambda i, h: (h, 0, i))
    shape = jax.ShapeDtypeStruct((heads, nkeys, t), F32)
    shape_bf16 = jax.ShapeDtypeStruct((heads, nkeys, t), BF16)
    rows = pl.BlockSpec((er, d), lambda i, h: (i * heads + h, 0))
    return pl.pallas_call(
        _route_kernel,
        grid=(t // tc, heads),
        in_specs=[pl.BlockSpec((tc, d), lambda i, h: (i, 0)),
                  pl.BlockSpec((d, 2 * half), lambda i, h: (0, 0)),
                  pl.BlockSpec((d, 2 * half), lambda i, h: (0, jnp.minimum(h + 1, heads - 1))),
                  pl.BlockSpec((1, 1, nkeys, half), lambda i, h: (h, 0, 0, 0)),
                  pl.BlockSpec((1, 1, nkeys, half), lambda i, h: (h, 1, 0, 0)),
                  rows, rows],
        out_specs=[out, out, out, out, rows, pl.BlockSpec((d, er), lambda i, h: (0, i * heads + h))],
        out_shape=[shape_bf16, shape_bf16, shape, shape,
                   jax.ShapeDtypeStruct((nexp, d), BF16), jax.ShapeDtypeStruct((d, nexp), BF16)],
        scratch_shapes=[pltpu.VMEM((2, tc, 2 * half), BF16)],
        compiler_params=_params("parallel", "arbitrary"),
        name="peer_route",
    )(x1b, wq, wq, keys, keys, u, v)


def _experts_kernel(xt_ref, u_ref, vt_ref, r2_ref, b2_ref, nb1_ref, a1_ref, y_ref, acc_ref, *, rows_per_step):
    s = pl.program_id(1)
    heads = r2_ref.shape[0]

    @pl.when(s == 0)
    def _():
        acc_ref[...] = jnp.zeros_like(acc_ref)

    nkeys = r2_ref.shape[1]
    sub = EXPERT_SUB_ROWS
    acts = []
    for q in range(rows_per_step // sub):
        wts = []
        for r in range(q * sub, (q + 1) * sub):
            i1 = s * rows_per_step + r
            wt = None
            for hd in range(heads):
                nb_row = nb1_ref[hd, pl.ds(i1, 1), :].astype(BF16)
                a_row = a1_ref[hd, pl.ds(i1, 1), :].astype(BF16)
                zero = jnp.zeros((), BF16)
                term = jnp.where(r2_ref[hd] < nb_row, b2_ref[hd], zero) * a_row
                wt = term if wt is None else wt + term
            wts.append(wt)
        w = jnp.concatenate(wts, axis=0)
        ht = _dot(u_ref[q * sub * nkeys:(q + 1) * sub * nkeys, :], xt_ref[...])
        two_gelu = ht * (1.0 + lax.erf(ht * SQRT_HALF))
        acts.append(w * two_gelu.astype(BF16))
    act = jnp.concatenate(acts, axis=0)
    acc_ref[...] += _dot(vt_ref[...], act)

    @pl.when(s == pl.num_programs(1) - 1)
    def _():
        y_ref[...] = acc_ref[...].T


def _experts(x1t, ub, vt, r2, b2, nb1, a1, *, tm, rows_per_step):
    d, t = x1t.shape
    heads, nkeys, _ = r2.shape
    eb = rows_per_step * nkeys
    route = pl.BlockSpec((heads, nkeys, tm), lambda i, s: (0, 0, i))
    return pl.pallas_call(
        functools.partial(_experts_kernel, rows_per_step=rows_per_step),
        grid=(t // tm, nkeys // rows_per_step),
        in_specs=[pl.BlockSpec((d, tm), lambda i, s: (0, i)),
                  pl.BlockSpec((eb, d), lambda i, s: (s, 0)),
                  pl.BlockSpec((d, eb), lambda i, s: (0, s)),
                  route, route, route, route],
        out_specs=pl.BlockSpec((tm, d), lambda i, s: (i, 0)),
        out_shape=jax.ShapeDtypeStruct((t, d), F32),
        scratch_shapes=[pltpu.VMEM((d, tm), F32)],
        compiler_params=_params("parallel", "arbitrary"),
        name="peer_experts",
    )(x1t, ub, vt, r2, b2, nb1, a1)


def _tail_kernel(x1_ref, y_ref, p_ref, g_ref, b_ref, wg_ref, wp_ref, o_ref, *, alpha):
    x2 = _layer_norm(alpha * x1_ref[...] + y_ref[...], g_ref[...], b_ref[...])
    gate = _sigmoid(_dot(x2.astype(BF16), wg_ref[...]))
    proj = _dot(p_ref[...].astype(BF16), wp_ref[...])
    o_ref[...] = x2 + gate * proj


def _tail(x1, y, p, g, b, wg, wp, *, alpha, tm):
    t, d = x1.shape
    pd = p.shape[1]
    row = pl.BlockSpec((tm, d), lambda i: (i, 0))
    vec = pl.BlockSpec((1, d), lambda i: (0, 0))
    return pl.pallas_call(
        functools.partial(_tail_kernel, alpha=alpha),
        grid=(t // tm,),
        in_specs=[row, row, pl.BlockSpec((tm, pd), lambda i: (i, 0)), vec, vec,
                  pl.BlockSpec((d, d), lambda i: (0, 0)), pl.BlockSpec((pd, d), lambda i: (0, 0))],
        out_specs=row,
        out_shape=jax.ShapeDtypeStruct((t, d), F32),
        compiler_params=_params("parallel"),
        name="ln2_ple",
    )(x1, y, p, g, b, wg, wp)


def kernel(x, p, w_in, b_in, gm_ln_g, gm_ln_b, gm_ws, gm_bs, w_gm_out, cv_w, cv_b, cv_ln_g, cv_ln_b,
           w_cv_out, w_o, ln1_g, ln1_b, peer_wq, peer_keys, peer_u, peer_v, ln2_g, ln2_b,
           ple_w_gate, ple_w_proj):
    bsz, seq, d = x.shape
    depth = w_in.shape[0]
    t = bsz * seq
    alpha = (2.0 * depth) ** 0.25
    groups, blk, _ = gm_ws.shape[1:]
    gdim = d // groups
    tm = min(512, seq)

    xf = x.reshape(t, d)
    for i in range(depth):
        row = lambda a: a[i].reshape(1, -1)
        bs_b = jnp.broadcast_to(gm_bs[i][:, :, None], (groups, blk, gdim))
        (sg, glu, ga, gb), (w_a, w_b, w_ob, wq_b, wg_b) = _inproj(
            xf, w_in[i].astype(BF16), row(b_in), row(gm_ln_g), row(gm_ln_b), gm_ws[i], bs_b,
            [w_gm_out[i], w_cv_out[i], w_o[i], peer_wq[i], ple_w_gate[i]], tm=min(1024, t), tn=gdim)
        m = _mix(sg, glu, cv_w[i], row(cv_b), row(cv_ln_g), row(cv_ln_b), w_a, w_b, ga, gb,
                 seq=seq, tm=tm, tn=512)
        x1, x1b, x1t = _oproj(m, xf, w_ob, row(ln1_g), row(ln1_b), alpha=alpha, tm=tm)
        r2, b2, nb1, a1, ub, vt = _route(x1b, wq_b, peer_keys[i], peer_u[i], peer_v[i], tc=min(1024, t))
        y = _experts(x1t, ub, vt, r2, b2, nb1, a1, tm=tm, rows_per_step=8)
        xf = _tail(x1, y, p[i].reshape(t, -1), row(ln2_g), row(ln2_b),
                   wg_b, ple_w_proj[i].astype(BF16), alpha=alpha, tm=tm)
    return xf.reshape(bsz, seq, d)
```

```python
import functools

import jax
import jax.numpy as jnp
from jax import lax
from jax.experimental import pallas as pl
from jax.experimental.pallas import tpu as pltpu

LN_EPS = 1e-5
CHUNK = 64
PEER_TOPK = 16
SQRT_HALF = 0.7071067811865476
LANES = 128
SUBLANES = 8
EXPERT_SUB_ROWS = 2
ROUTE_CHUNK = 256
HALO = 32
VMEM_LIMIT = 56 * 1024 * 1024

F32 = jnp.float32
BF16 = jnp.bfloat16


def _gelu(x):
    return 0.5 * x * (1.0 + lax.erf(x * SQRT_HALF))


def _sigmoid(x):
    return 1.0 / (1.0 + jnp.exp(-x))


def _layer_norm(x, g, b):
    mu = jnp.mean(x, axis=-1, keepdims=True)
    d = x - mu
    var = jnp.mean(d * d, axis=-1, keepdims=True)
    return d * lax.rsqrt(var + LN_EPS) * g + b


def _dot(a, b):
    return jnp.dot(a, b, preferred_element_type=F32)


def _params(*sem):
    return pltpu.CompilerParams(dimension_semantics=sem, vmem_limit_bytes=VMEM_LIMIT)


N_INPROJ_INPUTS = 17


def _inproj_kernel(*refs, n_cast):
    (x_ref, wu_ref, wv_ref, wa_ref, wb_ref, wga_ref, wgb_ref,
     bu_ref, bv_ref, ba_ref, bb_ref, bga_ref, bgb_ref,
     lng_ref, lnb_ref, ws_ref, bs_ref) = refs[:N_INPROJ_INPUTS]
    cast_in = refs[N_INPROJ_INPUTS:N_INPROJ_INPUTS + n_cast]
    sg_ref, glu_ref, ga_ref, gb_ref = refs[N_INPROJ_INPUTS + n_cast:N_INPROJ_INPUTS + n_cast + 4]
    cast_out = refs[N_INPROJ_INPUTS + n_cast + 4:N_INPROJ_INPUTS + 2 * n_cast + 4]
    xb_ref = refs[-1]

    for src, dst in zip(cast_in, cast_out):
        dst[...] = src[...].astype(BF16)

    @pl.when(pl.program_id(1) == 0)
    def _():
        xb_ref[...] = x_ref[...].astype(BF16)

    x = xb_ref[...]
    tm = x.shape[0]
    blk = ws_ref.shape[1]

    def proj(w_ref, b_ref):
        return _dot(x, w_ref[...]) + b_ref[...]

    u = _gelu(proj(wu_ref, bu_ref))
    v = _gelu(proj(wv_ref, bv_ref))
    vn = _layer_norm(v, lng_ref[...], lnb_ref[...]).astype(BF16)
    row = lax.broadcasted_iota(jnp.int32, (blk, blk), 0)
    col = lax.broadcasted_iota(jnp.int32, (blk, blk), 1)
    w = jnp.where(col // CHUNK <= row // CHUNK, ws_ref[0], 0.0).astype(BF16)
    bs = bs_ref[0]
    for nb in range(tm // blk):
        rows = slice(nb * blk, (nb + 1) * blk)
        mixed = _dot(w, vn[rows]) + bs
        sg_ref[rows, :] = (u[rows] * mixed).astype(BF16)

    glu_ref[...] = proj(wa_ref, ba_ref) * _sigmoid(proj(wb_ref, bb_ref))
    ga_ref[...] = _sigmoid(proj(wga_ref, bga_ref)).astype(BF16)
    gb_ref[...] = _sigmoid(proj(wgb_ref, bgb_ref)).astype(BF16)


def _inproj(x, w_in, b_in, ln_g, ln_b, ws, bs_b, later_weights, *, tm, tn):
    t, d = x.shape
    nseg = w_in.shape[1] // d
    ncol = d // tn
    steps = (t // tm) * ncol
    cast_specs, cast_shapes = [], []
    for w in later_weights:
        assert w.shape[0] % steps == 0 and (w.shape[0] // steps) % (2 * SUBLANES) == 0
        cast_specs.append(pl.BlockSpec((w.shape[0] // steps, w.shape[1]), lambda i, j: (i * ncol + j, 0)))
        cast_shapes.append(jax.ShapeDtypeStruct(w.shape, BF16))
    w_specs = [pl.BlockSpec((d, tn), functools.partial(lambda i, j, s: (0, s * ncol + j), s=s))
               for s in range(nseg)]
    b_specs = [pl.BlockSpec((1, tn), functools.partial(lambda i, j, s: (0, s * ncol + j), s=s))
               for s in range(nseg)]
    vec = pl.BlockSpec((1, tn), lambda i, j: (0, j))
    tile = pl.BlockSpec((tm, tn), lambda i, j: (i, j))
    in_specs = [pl.BlockSpec((tm, d), lambda i, j: (i, 0))] + w_specs + b_specs + [
        vec, vec,
        pl.BlockSpec((1,) + ws.shape[1:], lambda i, j: (j, 0, 0)),
        pl.BlockSpec((1,) + bs_b.shape[1:], lambda i, j: (j, 0, 0)),
    ]
    assert len(in_specs) == N_INPROJ_INPUTS
    outs = pl.pallas_call(
        functools.partial(_inproj_kernel, n_cast=len(later_weights)),
        grid=(t // tm, ncol),
        in_specs=in_specs + cast_specs,
        out_specs=[tile, tile, tile, tile] + cast_specs,
        out_shape=[jax.ShapeDtypeStruct((t, d), BF16), jax.ShapeDtypeStruct((t, d), F32),
                   jax.ShapeDtypeStruct((t, d), BF16), jax.ShapeDtypeStruct((t, d), BF16)] + cast_shapes,
        scratch_shapes=[pltpu.VMEM((tm, d), BF16)],
        compiler_params=_params("parallel", "arbitrary"),
        name="inproj",
    )(x, *([w_in] * nseg), *([b_in] * nseg), ln_g, ln_b, ws, bs_b, *later_weights)
    return outs[:4], outs[4:]


def _mix_kernel(sg_ref, glu_ref, halo_ref, cw_ref, cb_ref, clg_ref, clb_ref,
                wa_ref, wb_ref, ga_ref, gb_ref, m_ref, buf_ref, c_ref, cn_ref, *, blocks_per_seq):
    i = pl.program_id(0)
    j = pl.program_id(1)
    tm, d = glu_ref.shape
    taps = cw_ref.shape[0]

    @pl.when(j == 0)
    def _():
        keep = (i % blocks_per_seq != 0).astype(F32)
        buf_ref[0:HALO, :] = halo_ref[...] * keep
        buf_ref[HALO:HALO + tm, :] = glu_ref[...]
        buf_ref[HALO + tm:, :] = jnp.zeros((SUBLANES, d), F32)
        base = HALO - (taps - 1)

        def cols(c, carry):
            cs = pl.ds(pl.multiple_of(c * LANES, LANES), LANES)
            acc = jnp.broadcast_to(cb_ref[:, cs], (tm, LANES))
            for sh in range(SUBLANES):
                part = None
                for k in range(taps):
                    off = base + k
                    if off % SUBLANES == sh:
                        term = cw_ref[k:k + 1, cs] * buf_ref[off - sh:off - sh + tm + SUBLANES, cs]
                        part = term if part is None else part + term
                if part is not None:
                    acc = acc + part[sh:sh + tm]
            c_ref[:, cs] = acc
            return carry

        lax.fori_loop(0, d // LANES, cols, 0)
        c = _layer_norm(c_ref[...], clg_ref[...], clb_ref[...])
        cn_ref[...] = (c * _sigmoid(c)).astype(BF16)

    tn = m_ref.shape[1]
    cols = pl.ds(pl.multiple_of(j * tn, tn), tn)
    ya = _dot(sg_ref[...], wa_ref[:, cols])
    yb = _dot(cn_ref[...], wb_ref[:, cols])
    m_ref[...] = (ga_ref[...].astype(F32) * ya + gb_ref[...].astype(F32) * yb).astype(BF16)


def _mix(sg, glu, cv_w, cv_b, cv_ln_g, cv_ln_b, wa, wb, ga, gb, *, seq, tm, tn):
    t, d = glu.shape
    assert seq % tm == 0 and tm % HALO == 0 and cv_w.shape[0] - 1 <= HALO
    hb = tm // HALO
    row = pl.BlockSpec((tm, d), lambda i, j: (i, 0))
    vec = pl.BlockSpec((1, d), lambda i, j: (0, 0))
    wcol = pl.BlockSpec((d, d), lambda i, j: (0, 0), pipeline_mode=pl.Buffered(1))
    tile = pl.BlockSpec((tm, tn), lambda i, j: (i, j))
    return pl.pallas_call(
        functools.partial(_mix_kernel, blocks_per_seq=seq // tm),
        grid=(t // tm, d // tn),
        in_specs=[row, row,
                  pl.BlockSpec((HALO, d), lambda i, j: (jnp.maximum(i * hb - 1, 0), 0)),
                  pl.BlockSpec(cv_w.shape, lambda i, j: (0, 0)),
                  vec, vec, vec, wcol, wcol, tile, tile],
        out_specs=tile,
        out_shape=jax.ShapeDtypeStruct((t, d), BF16),
        scratch_shapes=[pltpu.VMEM((HALO + tm + SUBLANES, d), F32), pltpu.VMEM((tm, d), F32),
                        pltpu.VMEM((tm, d), BF16)],
        compiler_params=_params("parallel", "arbitrary"),
        name="branch_mix",
    )(sg, glu, glu, cv_w, cv_b, cv_ln_g, cv_ln_b, wa, wb, ga, gb)


def _oproj_kernel(m_ref, x_ref, wo_ref, g_ref, b_ref, x1_ref, x1b_ref, x1t_ref, *, alpha):
    mix = _dot(m_ref[...], wo_ref[...])
    x1 = _layer_norm(alpha * x_ref[...] + mix, g_ref[...], b_ref[...])
    x1_ref[...] = x1
    x1b_ref[...] = x1.astype(BF16)
    x1t_ref[...] = x1.T.astype(BF16)


def _oproj(m, x, wo, g, b, *, alpha, tm):
    t, d = x.shape
    row = pl.BlockSpec((tm, d), lambda i: (i, 0))
    vec = pl.BlockSpec((1, d), lambda i: (0, 0))
    return pl.pallas_call(
        functools.partial(_oproj_kernel, alpha=alpha),
        grid=(t // tm,),
        in_specs=[row, row, pl.BlockSpec((d, d), lambda i: (0, 0)), vec, vec],
        out_specs=[row, row, pl.BlockSpec((d, tm), lambda i: (0, i))],
        out_shape=[jax.ShapeDtypeStruct((t, d), F32), jax.ShapeDtypeStruct((t, d), BF16),
                   jax.ShapeDtypeStruct((d, t), BF16)],
        compiler_params=_params("parallel"),
        name="out_proj_ln1",
    )(m, x, wo, g, b)


def _top_k_rows(ss, k, tie_break):
    n = ss[0].shape[0]
    idx = lax.broadcasted_iota(jnp.int32, ss[0].shape, 0).astype(F32)
    ranks = [jnp.full(s.shape, float(k), F32) for s in ss]
    works = list(ss)
    tops = [[] for _ in ss]
    for r in range(k):
        for i in range(len(ss)):
            m = jnp.max(works[i], axis=0, keepdims=True)
            sel = works[i] == m
            if tie_break:
                first = jnp.min(jnp.where(sel, idx, float(n)), axis=0, keepdims=True)
                sel = idx == first
            ranks[i] = jnp.where(sel, float(r), ranks[i])
            works[i] = jnp.where(sel, -jnp.inf, works[i])
            tops[i].append(m)
    exact = None
    for rank in ranks:
        picked = jnp.sum((rank < float(k)).astype(F32), axis=0, keepdims=True)
        ok = jnp.all(picked == float(k))
        exact = ok if exact is None else jnp.logical_and(exact, ok)
    return tuple(ranks), tuple(jnp.concatenate(t, axis=0) for t in tops), exact


def _merge_top_k(t1, t2, k):
    aidx = lax.broadcasted_iota(jnp.int32, t1.shape, 0).astype(F32)
    nb = jnp.zeros(t1.shape, F32)
    head = t1 + t2[0:1]
    top = head[0:1]
    z = jnp.zeros_like(top)
    for _ in range(k):
        m = jnp.max(head, axis=0, keepdims=True)
        first = jnp.min(jnp.where(head == m, aidx, float(k)), axis=0, keepdims=True)
        sel = aidx == first
        z = z + jnp.exp(m - top)
        nb = jnp.where(sel, nb + 1.0, nb)
        nxt = jnp.sum(jnp.where(sel, nb, 0.0), axis=0, keepdims=True)
        t2n = jnp.sum(jnp.where(aidx == nxt, t2, 0.0), axis=0, keepdims=True)
        t2n = jnp.where(nxt >= float(k), -jnp.inf, t2n)
        head = jnp.where(sel, t1 + t2n, head)
    return nb, z


def _route_kernel(x_ref, wq0_ref, wqn_ref, k1_ref, k2_ref, u_ref, v_ref,
                  r2_ref, b2_ref, nb1_ref, a1_ref, ub_ref, vt_ref, q_ref):
    ub_ref[...] = u_ref[...].astype(BF16)
    vt_ref[...] = v_ref[...].T.astype(BF16)
    tc = x_ref.shape[0]
    half = k1_ref.shape[-1]
    k = PEER_TOPK
    h = pl.program_id(1)
    slot = h % 2

    @pl.when(h == 0)
    def _():
        q_ref[0] = _dot(x_ref[...], wq0_ref[...]).astype(BF16)

    k1 = k1_ref[0, 0].astype(BF16)
    k2 = k2_ref[0, 0].astype(BF16)
    nt = (((1,), (1,)), ((), ()))

    def chunk(c, tie_break):
        start = c * ROUTE_CHUNK
        ts = pl.ds(start if isinstance(c, int) else pl.multiple_of(start, ROUTE_CHUNK), ROUTE_CHUNK)
        qc = q_ref[slot, ts, :]
        s1 = lax.dot_general(k1, qc[:, :half], nt, preferred_element_type=F32)
        s2 = lax.dot_general(k2, qc[:, half:], nt, preferred_element_type=F32)
        (rank1, rank2), (t1, t2), exact = _top_k_rows((s1, s2), k, tie_break)
        nb, z = _merge_top_k(t1, t2, k)
        rank1 = rank1.astype(BF16)
        nb = nb.astype(BF16)
        nb1 = jnp.zeros(rank1.shape, BF16)
        for a in range(k):
            nb1 = jnp.where(rank1 == a, nb[a:a + 1], nb1)
        r2_ref[0, :, ts] = rank2.astype(BF16)
        b2_ref[0, :, ts] = jnp.exp(s2 - t2[0:1]).astype(BF16)
        nb1_ref[0, :, ts] = nb1.astype(F32)
        a1_ref[0, :, ts] = jnp.exp(s1 - t1[0:1]) * (0.5 / z)
        return exact

    exact = None
    for c in range(tc // ROUTE_CHUNK):
        ok = chunk(c, False)
        exact = ok if exact is None else jnp.logical_and(exact, ok)
    q_ref[1 - slot] = _dot(x_ref[...], wqn_ref[...]).astype(BF16)

    @pl.when(jnp.logical_not(exact))
    def _():
        def redo(c, carry):
            chunk(c, True)
            return carry
        lax.fori_loop(0, tc // ROUTE_CHUNK, redo, 0)


def _route(x1b, wq, keys, u, v, *, tc):
    t, d = x1b.shape
    heads, _, nkeys, half = keys.shape
    nexp = u.shape[0]
    steps = (t // tc) * heads
    assert nexp % steps == 0 and (nexp // steps) % LANES == 0
    er = nexp // steps
    out = pl.BlockSpec((1, nkeys, tc), lambda i, h: (h, 0, i))
    shape = jax.ShapeDtypeStruct((heads, nkeys, t), F32)
    shape_bf16 = jax.ShapeDtypeStruct((heads, nkeys, t), BF16)
    rows = pl.BlockSpec((er, d), lambda i, h: (i * heads + h, 0))
    return pl.pallas_call(
        _route_kernel,
        grid=(t // tc, heads),
        in_specs=[pl.BlockSpec((tc, d), lambda i, h: (i, 0)),
                  pl.BlockSpec((d, 2 * half), lambda i, h: (0, 0)),
                  pl.BlockSpec((d, 2 * half), lambda i, h: (0, jnp.minimum(h + 1, heads - 1))),
                  pl.BlockSpec((1, 1, nkeys, half), lambda i, h: (h, 0, 0, 0)),
                  pl.BlockSpec((1, 1, nkeys, half), lambda i, h: (h, 1, 0, 0)),
                  rows, rows],
        out_specs=[out, out, out, out, rows, pl.BlockSpec((d, er), lambda i, h: (0, i * heads + h))],
        out_shape=[shape_bf16, shape_bf16, shape, shape,
                   jax.ShapeDtypeStruct((nexp, d), BF16), jax.ShapeDtypeStruct((d, nexp), BF16)],
        scratch_shapes=[pltpu.VMEM((2, tc, 2 * half), BF16)],
        compiler_params=_params("parallel", "arbitrary"),
        name="peer_route",
    )(x1b, wq, wq, keys, keys, u, v)


def _experts_kernel(xt_ref, u_ref, vt_ref, r2_ref, b2_ref, nb1_ref, a1_ref, y_ref, acc_ref, *, rows_per_step):
    s = pl.program_id(1)
    heads = r2_ref.shape[0]

    @pl.when(s == 0)
    def _():
        acc_ref[...] = jnp.zeros_like(acc_ref)

    nkeys = r2_ref.shape[1]
    sub = EXPERT_SUB_ROWS
    acts = []
    for q in range(rows_per_step // sub):
        wts = []
        for r in range(q * sub, (q + 1) * sub):
            i1 = s * rows_per_step + r
            wt = None
            for hd in range(heads):
                nb_row = nb1_ref[hd, pl.ds(i1, 1), :].astype(BF16)
                a_row = a1_ref[hd, pl.ds(i1, 1), :].astype(BF16)
                zero = jnp.zeros((), BF16)
                term = jnp.where(r2_ref[hd] < nb_row, b2_ref[hd], zero) * a_row
                wt = term if wt is None else wt + term
            wts.append(wt)
        w = jnp.concatenate(wts, axis=0)
        ht = _dot(u_ref[q * sub * nkeys:(q + 1) * sub * nkeys, :], xt_ref[...])
        two_gelu = ht * (1.0 + lax.erf(ht * SQRT_HALF))
        acts.append(w * two_gelu.astype(BF16))
    act = jnp.concatenate(acts, axis=0)
    acc_ref[...] += _dot(vt_ref[...], act)

    @pl.when(s == pl.num_programs(1) - 1)
    def _():
        y_ref[...] = acc_ref[...].T


def _experts(x1t, ub, vt, r2, b2, nb1, a1, *, tm, rows_per_step):
    d, t = x1t.shape
    heads, nkeys, _ = r2.shape
    eb = rows_per_step * nkeys
    route = pl.BlockSpec((heads, nkeys, tm), lambda i, s: (0, 0, i))
    return pl.pallas_call(
        functools.partial(_experts_kernel, rows_per_step=rows_per_step),
        grid=(t // tm, nkeys // rows_per_step),
        in_specs=[pl.BlockSpec((d, tm), lambda i, s: (0, i)),
                  pl.BlockSpec((eb, d), lambda i, s: (s, 0)),
                  pl.BlockSpec((d, eb), lambda i, s: (0, s)),
                  route, route, route, route],
        out_specs=pl.BlockSpec((tm, d), lambda i, s: (i, 0)),
        out_shape=jax.ShapeDtypeStruct((t, d), F32),
        scratch_shapes=[pltpu.VMEM((d, tm), F32)],
        compiler_params=_params("parallel", "arbitrary"),
        name="peer_experts",
    )(x1t, ub, vt, r2, b2, nb1, a1)


def _tail_kernel(x1_ref, y_ref, p_ref, g_ref, b_ref, wg_ref, wp_ref, o_ref, *, alpha):
    x2 = _layer_norm(alpha * x1_ref[...] + y_ref[...], g_ref[...], b_ref[...])
    gate = _sigmoid(_dot(x2.astype(BF16), wg_ref[...]))
    proj = _dot(p_ref[...].astype(BF16), wp_ref[...])
    o_ref[...] = x2 + gate * proj


def _tail(x1, y, p, g, b, wg, wp, *, alpha, tm):
    t, d = x1.shape
    pd = p.shape[1]
    row = pl.BlockSpec((tm, d), lambda i: (i, 0))
    vec = pl.BlockSpec((1, d), lambda i: (0, 0))
    return pl.pallas_call(
        functools.partial(_tail_kernel, alpha=alpha),
        grid=(t // tm,),
        in_specs=[row, row, pl.BlockSpec((tm, pd), lambda i: (i, 0)), vec, vec,
                  pl.BlockSpec((d, d), lambda i: (0, 0)), pl.BlockSpec((pd, d), lambda i: (0, 0))],
        out_specs=row,
        out_shape=jax.ShapeDtypeStruct((t, d), F32),
        compiler_params=_params("parallel"),
        name="ln2_ple",
    )(x1, y, p, g, b, wg, wp)


def kernel(x, p, w_in, b_in, gm_ln_g, gm_ln_b, gm_ws, gm_bs, w_gm_out, cv_w, cv_b, cv_ln_g, cv_ln_b,
           w_cv_out, w_o, ln1_g, ln1_b, peer_wq, peer_keys, peer_u, peer_v, ln2_g, ln2_b,
           ple_w_gate, ple_w_proj):
    bsz, seq, d = x.shape
    depth = w_in.shape[0]
    t = bsz * seq
    alpha = (2.0 * depth) ** 0.25
    groups, blk, _ = gm_ws.shape[1:]
    gdim = d // groups
    tm = min(512, seq)

    xf = x.reshape(t, d)
    for i in range(depth):
        row = lambda a: a[i].reshape(1, -1)
        bs_b = jnp.broadcast_to(gm_bs[i][:, :, None], (groups, blk, gdim))
        (sg, glu, ga, gb), (w_a, w_b, w_ob, wq_b, wg_b) = _inproj(
            xf, w_in[i].astype(BF16), row(b_in), row(gm_ln_g), row(gm_ln_b), gm_ws[i], bs_b,
            [w_gm_out[i], w_cv_out[i], w_o[i], peer_wq[i], ple_w_gate[i]], tm=min(1024, t), tn=gdim)
        m = _mix(sg, glu, cv_w[i], row(cv_b), row(cv_ln_g), row(cv_ln_b), w_a, w_b, ga, gb,
                 seq=seq, tm=tm, tn=512)
        x1, x1b, x1t = _oproj(m, xf, w_ob, row(ln1_g), row(ln1_b), alpha=alpha, tm=tm)
        r2, b2, nb1, a1, ub, vt = _route(x1b, wq_b, peer_keys[i], peer_u[i], peer_v[i], tc=min(1024, t))
        y = _experts(x1t, ub, vt, r2, b2, nb1, a1, tm=tm, rows_per_step=8)
        xf = _tail(x1, y, p[i].reshape(t, -1), row(ln2_g), row(ln2_b),
                   wg_b, ple_w_proj[i].astype(BF16), alpha=alpha, tm=tm)
    return xf.reshape(bsz, seq, d)
```

```python
import functools

import jax
import jax.numpy as jnp
from jax import lax
from jax.experimental import pallas as pl
from jax.experimental.pallas import tpu as pltpu

LN_EPS = 1e-5
CHUNK = 64
PEER_TOPK = 16
SQRT_HALF = 0.7071067811865476
LANES = 128
SUBLANES = 8
EXPERT_SUB_ROWS = 2
ROUTE_CHUNK = 256
HALO = 32
VMEM_LIMIT = 56 * 1024 * 1024

F32 = jnp.float32
BF16 = jnp.bfloat16


def _gelu(x):
    return 0.5 * x * (1.0 + lax.erf(x * SQRT_HALF))


def _sigmoid(x):
    return 1.0 / (1.0 + jnp.exp(-x))


def _layer_norm(x, g, b):
    mu = jnp.mean(x, axis=-1, keepdims=True)
    d = x - mu
    var = jnp.mean(d * d, axis=-1, keepdims=True)
    return d * lax.rsqrt(var + LN_EPS) * g + b


def _dot(a, b):
    return jnp.dot(a, b, preferred_element_type=F32)


def _params(*sem):
    return pltpu.CompilerParams(dimension_semantics=sem, vmem_limit_bytes=VMEM_LIMIT)


N_INPROJ_INPUTS = 17


def _inproj_kernel(*refs, n_cast):
    (x_ref, wu_ref, wv_ref, wa_ref, wb_ref, wga_ref, wgb_ref,
     bu_ref, bv_ref, ba_ref, bb_ref, bga_ref, bgb_ref,
     lng_ref, lnb_ref, ws_ref, bs_ref) = refs[:N_INPROJ_INPUTS]
    cast_in = refs[N_INPROJ_INPUTS:N_INPROJ_INPUTS + n_cast]
    sg_ref, glu_ref, ga_ref, gb_ref = refs[N_INPROJ_INPUTS + n_cast:N_INPROJ_INPUTS + n_cast + 4]
    cast_out = refs[N_INPROJ_INPUTS + n_cast + 4:N_INPROJ_INPUTS + 2 * n_cast + 4]
    xb_ref = refs[-1]

    for src, dst in zip(cast_in, cast_out):
        dst[...] = src[...].astype(BF16)

    @pl.when(pl.program_id(1) == 0)
    def _():
        xb_ref[...] = x_ref[...].astype(BF16)

    x = xb_ref[...]
    tm = x.shape[0]
    blk = ws_ref.shape[1]

    def proj(w_ref, b_ref):
        return _dot(x, w_ref[...]) + b_ref[...]

    u = _gelu(proj(wu_ref, bu_ref))
    v = _gelu(proj(wv_ref, bv_ref))
    vn = _layer_norm(v, lng_ref[...], lnb_ref[...]).astype(BF16)
    row = lax.broadcasted_iota(jnp.int32, (blk, blk), 0)
    col = lax.broadcasted_iota(jnp.int32, (blk, blk), 1)
    w = jnp.where(col // CHUNK <= row // CHUNK, ws_ref[0], 0.0).astype(BF16)
    bs = bs_ref[0]
    for nb in range(tm // blk):
        rows = slice(nb * blk, (nb + 1) * blk)
        mixed = _dot(w, vn[rows]) + bs
        sg_ref[rows, :] = (u[rows] * mixed).astype(BF16)

    glu_ref[...] = proj(wa_ref, ba_ref) * _sigmoid(proj(wb_ref, bb_ref))
    ga_ref[...] = _sigmoid(proj(wga_ref, bga_ref)).astype(BF16)
    gb_ref[...] = _sigmoid(proj(wgb_ref, bgb_ref)).astype(BF16)


def _inproj(x, w_in, b_in, ln_g, ln_b, ws, bs_b, later_weights, *, tm, tn):
    t, d = x.shape
    nseg = w_in.shape[1] // d
    ncol = d // tn
    steps = (t // tm) * ncol
    cast_specs, cast_shapes = [], []
    for w in later_weights:
        assert w.shape[0] % steps == 0 and (w.shape[0] // steps) % (2 * SUBLANES) == 0
        cast_specs.append(pl.BlockSpec((w.shape[0] // steps, w.shape[1]), lambda i, j: (i * ncol + j, 0)))
        cast_shapes.append(jax.ShapeDtypeStruct(w.shape, BF16))
    w_specs = [pl.BlockSpec((d, tn), functools.partial(lambda i, j, s: (0, s * ncol + j), s=s))
               for s in range(nseg)]
    b_specs = [pl.BlockSpec((1, tn), functools.partial(lambda i, j, s: (0, s * ncol + j), s=s))
               for s in range(nseg)]
    vec = pl.BlockSpec((1, tn), lambda i, j: (0, j))
    tile = pl.BlockSpec((tm, tn), lambda i, j: (i, j))
    in_specs = [pl.BlockSpec((tm, d), lambda i, j: (i, 0))] + w_specs + b_specs + [
        vec, vec,
        pl.BlockSpec((1,) + ws.shape[1:], lambda i, j: (j, 0, 0)),
        pl.BlockSpec((1,) + bs_b.shape[1:], lambda i, j: (j, 0, 0)),
    ]
    assert len(in_specs) == N_INPROJ_INPUTS
    outs = pl.pallas_call(
        functools.partial(_inproj_kernel, n_cast=len(later_weights)),
        grid=(t // tm, ncol),
        in_specs=in_specs + cast_specs,
        out_specs=[tile, tile, tile, tile] + cast_specs,
        out_shape=[jax.ShapeDtypeStruct((t, d), BF16), jax.ShapeDtypeStruct((t, d), F32),
                   jax.ShapeDtypeStruct((t, d), BF16), jax.ShapeDtypeStruct((t, d), BF16)] + cast_shapes,
        scratch_shapes=[pltpu.VMEM((tm, d), BF16)],
        compiler_params=_params("parallel", "arbitrary"),
        name="inproj",
    )(x, *([w_in] * nseg), *([b_in] * nseg), ln_g, ln_b, ws, bs_b, *later_weights)
    return outs[:4], outs[4:]


def _mix_kernel(sg_ref, glu_ref, halo_ref, cw_ref, cb_ref, clg_ref, clb_ref,
                wa_ref, wb_ref, ga_ref, gb_ref, m_ref, buf_ref, c_ref, cn_ref, *, blocks_per_seq):
    i = pl.program_id(0)
    j = pl.program_id(1)
    tm, d = glu_ref.shape
    taps = cw_ref.shape[0]

    @pl.when(j == 0)
    def _():
        keep = (i % blocks_per_seq != 0).astype(F32)
        buf_ref[0:HALO, :] = halo_ref[...] * keep
        buf_ref[HALO:HALO + tm, :] = glu_ref[...]
        buf_ref[HALO + tm:, :] = jnp.zeros((SUBLANES, d), F32)
        base = HALO - (taps - 1)

        def cols(c, carry):
            cs = pl.ds(pl.multiple_of(c * LANES, LANES), LANES)
            acc = jnp.broadcast_to(cb_ref[:, cs], (tm, LANES))
            for sh in range(SUBLANES):
                part = None
                for k in range(taps):
                    off = base + k
                    if off % SUBLANES == sh:
                        term = cw_ref[k:k + 1, cs] * buf_ref[off - sh:off - sh + tm + SUBLANES, cs]
                        part = term if part is None else part + term
                if part is not None:
                    acc = acc + part[sh:sh + tm]
            c_ref[:, cs] = acc
            return carry

        lax.fori_loop(0, d // LANES, cols, 0)
        c = _layer_norm(c_ref[...], clg_ref[...], clb_ref[...])
        cn_ref[...] = (c * _sigmoid(c)).astype(BF16)

    tn = m_ref.shape[1]
    cols = pl.ds(pl.multiple_of(j * tn, tn), tn)
    ya = _dot(sg_ref[...], wa_ref[:, cols])
    yb = _dot(cn_ref[...], wb_ref[:, cols])
    m_ref[...] = (ga_ref[...].astype(F32) * ya + gb_ref[...].astype(F32) * yb).astype(BF16)


def _mix(sg, glu, cv_w, cv_b, cv_ln_g, cv_ln_b, wa, wb, ga, gb, *, seq, tm, tn):
    t, d = glu.shape
    assert seq % tm == 0 and tm % HALO == 0 and cv_w.shape[0] - 1 <= HALO
    hb = tm // HALO
    row = pl.BlockSpec((tm, d), lambda i, j: (i, 0))
    vec = pl.BlockSpec((1, d), lambda i, j: (0, 0))
    wcol = pl.BlockSpec((d, d), lambda i, j: (0, 0), pipeline_mode=pl.Buffered(1))
    tile = pl.BlockSpec((tm, tn), lambda i, j: (i, j))
    return pl.pallas_call(
        functools.partial(_mix_kernel, blocks_per_seq=seq // tm),
        grid=(t // tm, d // tn),
        in_specs=[row, row,
                  pl.BlockSpec((HALO, d), lambda i, j: (jnp.maximum(i * hb - 1, 0), 0)),
                  pl.BlockSpec(cv_w.shape, lambda i, j: (0, 0)),
                  vec, vec, vec, wcol, wcol, tile, tile],
        out_specs=tile,
        out_shape=jax.ShapeDtypeStruct((t, d), BF16),
        scratch_shapes=[pltpu.VMEM((HALO + tm + SUBLANES, d), F32), pltpu.VMEM((tm, d), F32),
                        pltpu.VMEM((tm, d), BF16)],
        compiler_params=_params("parallel", "arbitrary"),
        name="branch_mix",
    )(sg, glu, glu, cv_w, cv_b, cv_ln_g, cv_ln_b, wa, wb, ga, gb)


def _oproj_kernel(m_ref, x_ref, wo_ref, g_ref, b_ref, x1_ref, x1b_ref, x1t_ref, *, alpha):
    mix = _dot(m_ref[...], wo_ref[...])
    x1 = _layer_norm(alpha * x_ref[...] + mix, g_ref[...], b_ref[...])
    x1_ref[...] = x1
    x1b_ref[...] = x1.astype(BF16)
    x1t_ref[...] = x1.T.astype(BF16)


def _oproj(m, x, wo, g, b, *, alpha, tm):
    t, d = x.shape
    row = pl.BlockSpec((tm, d), lambda i: (i, 0))
    vec = pl.BlockSpec((1, d), lambda i: (0, 0))
    return pl.pallas_call(
        functools.partial(_oproj_kernel, alpha=alpha),
        grid=(t // tm,),
        in_specs=[row, row, pl.BlockSpec((d, d), lambda i: (0, 0)), vec, vec],
        out_specs=[row, row, pl.BlockSpec((d, tm), lambda i: (0, i))],
        out_shape=[jax.ShapeDtypeStruct((t, d), F32), jax.ShapeDtypeStruct((t, d), BF16),
                   jax.ShapeDtypeStruct((d, t), BF16)],
        compiler_params=_params("parallel"),
        name="out_proj_ln1",
    )(m, x, wo, g, b)


def _top_k_rows(ss, k, tie_break):
    n = ss[0].shape[0]
    idx = lax.broadcasted_iota(jnp.int32, ss[0].shape, 0).astype(F32)
    ranks = [jnp.full(s.shape, float(k), F32) for s in ss]
    works = list(ss)
    tops = [[] for _ in ss]
    for r in range(k):
        for i in range(len(ss)):
            m = jnp.max(works[i], axis=0, keepdims=True)
            sel = works[i] == m
            if tie_break:
                first = jnp.min(jnp.where(sel, idx, float(n)), axis=0, keepdims=True)
                sel = idx == first
            ranks[i] = jnp.where(sel, float(r), ranks[i])
            works[i] = jnp.where(sel, -jnp.inf, works[i])
            tops[i].append(m)
    exact = None
    for rank in ranks:
        picked = jnp.sum((rank < float(k)).astype(F32), axis=0, keepdims=True)
        ok = jnp.all(picked == float(k))
        exact = ok if exact is None else jnp.logical_and(exact, ok)
    return tuple(ranks), tuple(jnp.concatenate(t, axis=0) for t in tops), exact


def _merge_top_k(t1, t2, k):
    aidx = lax.broadcasted_iota(jnp.int32, t1.shape, 0).astype(F32)
    nb = jnp.zeros(t1.shape, F32)
    head = t1 + t2[0:1]
    top = head[0:1]
    z = jnp.zeros_like(top)
    for _ in range(k):
        m = jnp.max(head, axis=0, keepdims=True)
        first = jnp.min(jnp.where(head == m, aidx, float(k)), axis=0, keepdims=True)
        sel = aidx == first
        z = z + jnp.exp(m - top)
        nb = jnp.where(sel, nb + 1.0, nb)
        nxt = jnp.sum(jnp.where(sel, nb, 0.0), axis=0, keepdims=True)
        t2n = jnp.sum(jnp.where(aidx == nxt, t2, 0.0), axis=0, keepdims=True)
        t2n = jnp.where(nxt >= float(k), -jnp.inf, t2n)
        head = jnp.where(sel, t1 + t2n, head)
    return nb, z


def _route_kernel(x_ref, wq0_ref, wqn_ref, k1_ref, k2_ref, u_ref, v_ref,
                  r2_ref, b2_ref, nb1_ref, a1_ref, ub_ref, vt_ref, q_ref, exact_ref):
    ub_ref[...] = u_ref[...].astype(BF16)
    vt_ref[...] = v_ref[...].T.astype(BF16)
    tc = x_ref.shape[0]
    half = k1_ref.shape[-1]
    k = PEER_TOPK
    h = pl.program_id(1)
    slot = h % 2

    @pl.when(h == 0)
    def _():
        q_ref[0] = _dot(x_ref[...], wq0_ref[...]).astype(BF16)

    k1 = k1_ref[0, 0].astype(BF16)
    k2 = k2_ref[0, 0].astype(BF16)
    nt = (((1,), (1,)), ((), ()))

    def chunk(c, tie_break):
        start = c * ROUTE_CHUNK
        ts = pl.ds(start if isinstance(c, int) else pl.multiple_of(start, ROUTE_CHUNK), ROUTE_CHUNK)
        qc = q_ref[slot, ts, :]
        s1 = lax.dot_general(k1, qc[:, :half], nt, preferred_element_type=F32)
        s2 = lax.dot_general(k2, qc[:, half:], nt, preferred_element_type=F32)
        (rank1, rank2), (t1, t2), exact = _top_k_rows((s1, s2), k, tie_break)
        nb, z = _merge_top_k(t1, t2, k)
        rank1 = rank1.astype(BF16)
        nb = nb.astype(BF16)
        nb1 = jnp.zeros(rank1.shape, BF16)
        for a in range(k):
            nb1 = jnp.where(rank1 == a, nb[a:a + 1], nb1)
        r2_ref[0, :, ts] = rank2.astype(BF16)
        b2_ref[0, :, ts] = jnp.exp(s2 - t2[0:1]).astype(BF16)
        nb1_ref[0, :, ts] = nb1.astype(F32)
        a1_ref[0, :, ts] = jnp.exp(s1 - t1[0:1]) * (0.5 / z)
        return exact

    for c in range(tc // ROUTE_CHUNK):
        exact_ref[c] = chunk(c, False).astype(jnp.int32)
    q_ref[1 - slot] = _dot(x_ref[...], wqn_ref[...]).astype(BF16)

    def redo(c, carry):
        @pl.when(exact_ref[c] == 0)
        def _():
            chunk(c, True)
        return carry

    lax.fori_loop(0, tc // ROUTE_CHUNK, redo, 0)


def _route(x1b, wq, keys, u, v, *, tc):
    t, d = x1b.shape
    heads, _, nkeys, half = keys.shape
    nexp = u.shape[0]
    steps = (t // tc) * heads
    assert nexp % steps == 0 and (nexp // steps) % LANES == 0
    er = nexp // steps
    out = pl.BlockSpec((1, nkeys, tc), lambda i, h: (h, 0, i))
    shape = jax.ShapeDtypeStruct((heads, nkeys, t), F32)
    shape_bf16 = jax.ShapeDtypeStruct((heads, nkeys, t), BF16)
    rows = pl.BlockSpec((er, d), lambda i, h: (i * heads + h, 0))
    return pl.pallas_call(
        _route_kernel,
        grid=(t // tc, heads),
        in_specs=[pl.BlockSpec((tc, d), lambda i, h: (i, 0)),
                  pl.BlockSpec((d, 2 * half), lambda i, h: (0, 0)),
                  pl.BlockSpec((d, 2 * half), lambda i, h: (0, jnp.minimum(h + 1, heads - 1))),
                  pl.BlockSpec((1, 1, nkeys, half), lambda i, h: (h, 0, 0, 0)),
                  pl.BlockSpec((1, 1, nkeys, half), lambda i, h: (h, 1, 0, 0)),
                  rows, rows],
        out_specs=[out, out, out, out, rows, pl.BlockSpec((d, er), lambda i, h: (0, i * heads + h))],
        out_shape=[shape_bf16, shape_bf16, shape, shape,
                   jax.ShapeDtypeStruct((nexp, d), BF16), jax.ShapeDtypeStruct((d, nexp), BF16)],
        scratch_shapes=[pltpu.VMEM((2, tc, 2 * half), BF16), pltpu.SMEM((tc // ROUTE_CHUNK,), jnp.int32)],
        compiler_params=_params("parallel", "arbitrary"),
        name="peer_route",
    )(x1b, wq, wq, keys, keys, u, v)


def _experts_kernel(xt_ref, u_ref, vt_ref, r2_ref, b2_ref, nb1_ref, a1_ref, y_ref, acc_ref, *, rows_per_step):
    s = pl.program_id(1)
    heads = r2_ref.shape[0]

    @pl.when(s == 0)
    def _():
        acc_ref[...] = jnp.zeros_like(acc_ref)

    nkeys = r2_ref.shape[1]
    sub = EXPERT_SUB_ROWS
    acts = []
    for q in range(rows_per_step // sub):
        wts = []
        for r in range(q * sub, (q + 1) * sub):
            i1 = s * rows_per_step + r
            wt = None
            for hd in range(heads):
                nb_row = nb1_ref[hd, pl.ds(i1, 1), :].astype(BF16)
                a_row = a1_ref[hd, pl.ds(i1, 1), :].astype(BF16)
                zero = jnp.zeros((), BF16)
                term = jnp.where(r2_ref[hd] < nb_row, b2_ref[hd], zero) * a_row
                wt = term if wt is None else wt + term
            wts.append(wt)
        w = jnp.concatenate(wts, axis=0)
        ht = _dot(u_ref[q * sub * nkeys:(q + 1) * sub * nkeys, :], xt_ref[...])
        two_gelu = ht * (1.0 + lax.erf(ht * SQRT_HALF))
        acts.append(w * two_gelu.astype(BF16))
    act = jnp.concatenate(acts, axis=0)
    acc_ref[...] += _dot(vt_ref[...], act)

    @pl.when(s == pl.num_programs(1) - 1)
    def _():
        y_ref[...] = acc_ref[...].T


def _experts(x1t, ub, vt, r2, b2, nb1, a1, *, tm, rows_per_step):
    d, t = x1t.shape
    heads, nkeys, _ = r2.shape
    eb = rows_per_step * nkeys
    route = pl.BlockSpec((heads, nkeys, tm), lambda i, s: (0, 0, i))
    return pl.pallas_call(
        functools.partial(_experts_kernel, rows_per_step=rows_per_step),
        grid=(t // tm, nkeys // rows_per_step),
        in_specs=[pl.BlockSpec((d, tm), lambda i, s: (0, i)),
                  pl.BlockSpec((eb, d), lambda i, s: (s, 0)),
                  pl.BlockSpec((d, eb), lambda i, s: (0, s)),
                  route, route, route, route],
        out_specs=pl.BlockSpec((tm, d), lambda i, s: (i, 0)),
        out_shape=jax.ShapeDtypeStruct((t, d), F32),
        scratch_shapes=[pltpu.VMEM((d, tm), F32)],
        compiler_params=_params("parallel", "arbitrary"),
        name="peer_experts",
    )(x1t, ub, vt, r2, b2, nb1, a1)


def _tail_kernel(x1_ref, y_ref, p_ref, g_ref, b_ref, wg_ref, wp_ref, o_ref, *, alpha):
    x2 = _layer_norm(alpha * x1_ref[...] + y_ref[...], g_ref[...], b_ref[...])
    gate = _sigmoid(_dot(x2.astype(BF16), wg_ref[...]))
    proj = _dot(p_ref[...].astype(BF16), wp_ref[...])
    o_ref[...] = x2 + gate * proj


def _tail(x1, y, p, g, b, wg, wp, *, alpha, tm):
    t, d = x1.shape
    pd = p.shape[1]
    row = pl.BlockSpec((tm, d), lambda i: (i, 0))
    vec = pl.BlockSpec((1, d), lambda i: (0, 0))
    return pl.pallas_call(
        functools.partial(_tail_kernel, alpha=alpha),
        grid=(t // tm,),
        in_specs=[row, row, pl.BlockSpec((tm, pd), lambda i: (i, 0)), vec, vec,
                  pl.BlockSpec((d, d), lambda i: (0, 0)), pl.BlockSpec((pd, d), lambda i: (0, 0))],
        out_specs=row,
        out_shape=jax.ShapeDtypeStruct((t, d), F32),
        compiler_params=_params("parallel"),
        name="ln2_ple",
    )(x1, y, p, g, b, wg, wp)


def kernel(x, p, w_in, b_in, gm_ln_g, gm_ln_b, gm_ws, gm_bs, w_gm_out, cv_w, cv_b, cv_ln_g, cv_ln_b,
           w_cv_out, w_o, ln1_g, ln1_b, peer_wq, peer_keys, peer_u, peer_v, ln2_g, ln2_b,
           ple_w_gate, ple_w_proj):
    bsz, seq, d = x.shape
    depth = w_in.shape[0]
    t = bsz * seq
    alpha = (2.0 * depth) ** 0.25
    groups, blk, _ = gm_ws.shape[1:]
    gdim = d // groups
    tm = min(512, seq)

    xf = x.reshape(t, d)
    for i in range(depth):
        row = lambda a: a[i].reshape(1, -1)
        bs_b = jnp.broadcast_to(gm_bs[i][:, :, None], (groups, blk, gdim))
        (sg, glu, ga, gb), (w_a, w_b, w_ob, wq_b, wg_b) = _inproj(
            xf, w_in[i].astype(BF16), row(b_in), row(gm_ln_g), row(gm_ln_b), gm_ws[i], bs_b,
            [w_gm_out[i], w_cv_out[i], w_o[i], peer_wq[i], ple_w_gate[i]], tm=min(1024, t), tn=gdim)
        m = _mix(sg, glu, cv_w[i], row(cv_b), row(cv_ln_g), row(cv_ln_b), w_a, w_b, ga, gb,
                 seq=seq, tm=tm, tn=512)
        x1, x1b, x1t = _oproj(m, xf, w_ob, row(ln1_g), row(ln1_b), alpha=alpha, tm=tm)
        r2, b2, nb1, a1, ub, vt = _route(x1b, wq_b, peer_keys[i], peer_u[i], peer_v[i], tc=min(1024, t))
        y = _experts(x1t, ub, vt, r2, b2, nb1, a1, tm=tm, rows_per_step=8)
        xf = _tail(x1, y, p[i].reshape(t, -1), row(ln2_g), row(ln2_b),
                   wg_b, ple_w_proj[i].astype(BF16), alpha=alpha, tm=tm)
    return xf.reshape(bsz, seq, d)
```

```python
import functools

import jax
import jax.numpy as jnp
from jax import lax
from jax.experimental import pallas as pl
from jax.experimental.pallas import tpu as pltpu

LN_EPS = 1e-5
CHUNK = 64
PEER_TOPK = 16
SQRT_HALF = 0.7071067811865476
LANES = 128
SUBLANES = 8
EXPERT_SUB_ROWS = 2
ROUTE_CHUNK = 256
HALO = 32
VMEM_LIMIT = 56 * 1024 * 1024

F32 = jnp.float32
BF16 = jnp.bfloat16


def _gelu(x):
    return 0.5 * x * (1.0 + lax.erf(x * SQRT_HALF))


def _sigmoid(x):
    return 1.0 / (1.0 + jnp.exp(-x))


def _layer_norm(x, g, b):
    mu = jnp.mean(x, axis=-1, keepdims=True)
    d = x - mu
    var = jnp.mean(d * d, axis=-1, keepdims=True)
    return d * lax.rsqrt(var + LN_EPS) * g + b


def _dot(a, b):
    return jnp.dot(a, b, preferred_element_type=F32)


def _params(*sem):
    return pltpu.CompilerParams(dimension_semantics=sem, vmem_limit_bytes=VMEM_LIMIT)


N_INPROJ_INPUTS = 10


def _inproj_kernel(*refs, n_cast):
    (x_ref, wu_ref, wv_ref, wa_ref, wb_ref, wga_ref, wgb_ref, vec_ref, ws_ref, bs_ref) = refs[:N_INPROJ_INPUTS]
    bu_ref, bv_ref, ba_ref, bb_ref, bga_ref, bgb_ref, lng_ref, lnb_ref = (
        vec_ref.at[r:r + 1, :] for r in range(8))
    cast_in = refs[N_INPROJ_INPUTS:N_INPROJ_INPUTS + n_cast]
    sg_ref, glu_ref, ga_ref, gb_ref = refs[N_INPROJ_INPUTS + n_cast:N_INPROJ_INPUTS + n_cast + 4]
    cast_out = refs[N_INPROJ_INPUTS + n_cast + 4:N_INPROJ_INPUTS + 2 * n_cast + 4]
    xb_ref = refs[-1]

    for src, dst in zip(cast_in, cast_out):
        dst[...] = src[...].astype(BF16)

    @pl.when(pl.program_id(1) == 0)
    def _():
        xb_ref[...] = x_ref[...].astype(BF16)

    x = xb_ref[...]
    tm = x.shape[0]
    blk = ws_ref.shape[1]

    def proj(w_ref, b_ref):
        return _dot(x, w_ref[...]) + b_ref[...]

    u = _gelu(proj(wu_ref, bu_ref))
    v = _gelu(proj(wv_ref, bv_ref))
    vn = _layer_norm(v, lng_ref[...], lnb_ref[...]).astype(BF16)
    row = lax.broadcasted_iota(jnp.int32, (blk, blk), 0)
    col = lax.broadcasted_iota(jnp.int32, (blk, blk), 1)
    w = jnp.where(col // CHUNK <= row // CHUNK, ws_ref[0], 0.0).astype(BF16)
    bs = bs_ref[0]
    for nb in range(tm // blk):
        rows = slice(nb * blk, (nb + 1) * blk)
        mixed = _dot(w, vn[rows]) + bs
        sg_ref[rows, :] = (u[rows] * mixed).astype(BF16)

    glu_ref[...] = proj(wa_ref, ba_ref) * _sigmoid(proj(wb_ref, bb_ref))
    ga_ref[...] = _sigmoid(proj(wga_ref, bga_ref)).astype(BF16)
    gb_ref[...] = _sigmoid(proj(wgb_ref, bgb_ref)).astype(BF16)


def _inproj(x, w_in, b_in, ln_g, ln_b, ws, bs_b, later_weights, *, tm, tn):
    t, d = x.shape
    nseg = w_in.shape[1] // d
    ncol = d // tn
    steps = (t // tm) * ncol
    cast_specs, cast_shapes = [], []
    for w in later_weights:
        assert w.shape[0] % steps == 0 and (w.shape[0] // steps) % (2 * SUBLANES) == 0
        cast_specs.append(pl.BlockSpec((w.shape[0] // steps, w.shape[1]), lambda i, j: (i * ncol + j, 0)))
        cast_shapes.append(jax.ShapeDtypeStruct(w.shape, BF16))
    w_specs = [pl.BlockSpec((d, tn), functools.partial(lambda i, j, s: (0, s * ncol + j), s=s))
               for s in range(nseg)]
    vecs = jnp.concatenate([b_in.reshape(nseg, d), ln_g, ln_b], axis=0)
    tile = pl.BlockSpec((tm, tn), lambda i, j: (i, j))
    in_specs = [pl.BlockSpec((tm, d), lambda i, j: (i, 0))] + w_specs + [
        pl.BlockSpec((vecs.shape[0], tn), lambda i, j: (0, j)),
        pl.BlockSpec((1,) + ws.shape[1:], lambda i, j: (j, 0, 0)),
        pl.BlockSpec((1,) + bs_b.shape[1:], lambda i, j: (j, 0, 0)),
    ]
    assert len(in_specs) == N_INPROJ_INPUTS
    outs = pl.pallas_call(
        functools.partial(_inproj_kernel, n_cast=len(later_weights)),
        grid=(t // tm, ncol),
        in_specs=in_specs + cast_specs,
        out_specs=[tile, tile, tile, tile] + cast_specs,
        out_shape=[jax.ShapeDtypeStruct((t, d), BF16), jax.ShapeDtypeStruct((t, d), F32),
                   jax.ShapeDtypeStruct((t, d), BF16), jax.ShapeDtypeStruct((t, d), BF16)] + cast_shapes,
        scratch_shapes=[pltpu.VMEM((tm, d), BF16)],
        compiler_params=_params("parallel", "arbitrary"),
        name="inproj",
    )(x, *([w_in] * nseg), vecs, ws, bs_b, *later_weights)
    return outs[:4], outs[4:]


def _mix_kernel(sg_ref, glu_ref, halo_ref, cw_ref, cb_ref, clg_ref, clb_ref,
                wa_ref, wb_ref, ga_ref, gb_ref, m_ref, buf_ref, c_ref, cn_ref, *, blocks_per_seq):
    i = pl.program_id(0)
    j = pl.program_id(1)
    tm, d = glu_ref.shape
    taps = cw_ref.shape[0]

    @pl.when(j == 0)
    def _():
        keep = (i % blocks_per_seq != 0).astype(F32)
        buf_ref[0:HALO, :] = halo_ref[...] * keep
        buf_ref[HALO:HALO + tm, :] = glu_ref[...]
        buf_ref[HALO + tm:, :] = jnp.zeros((SUBLANES, d), F32)
        base = HALO - (taps - 1)

        def cols(c, carry):
            cs = pl.ds(pl.multiple_of(c * LANES, LANES), LANES)
            acc = jnp.broadcast_to(cb_ref[:, cs], (tm, LANES))
            for sh in range(SUBLANES):
                part = None
                for k in range(taps):
                    off = base + k
                    if off % SUBLANES == sh:
                        term = cw_ref[k:k + 1, cs] * buf_ref[off - sh:off - sh + tm + SUBLANES, cs]
                        part = term if part is None else part + term
                if part is not None:
                    acc = acc + part[sh:sh + tm]
            c_ref[:, cs] = acc
            return carry

        lax.fori_loop(0, d // LANES, cols, 0)
        c = _layer_norm(c_ref[...], clg_ref[...], clb_ref[...])
        cn_ref[...] = (c * _sigmoid(c)).astype(BF16)

    tn = m_ref.shape[1]
    cols = pl.ds(pl.multiple_of(j * tn, tn), tn)
    ya = _dot(sg_ref[...], wa_ref[:, cols])
    yb = _dot(cn_ref[...], wb_ref[:, cols])
    m_ref[...] = (ga_ref[...].astype(F32) * ya + gb_ref[...].astype(F32) * yb).astype(BF16)


def _mix(sg, glu, cv_w, cv_b, cv_ln_g, cv_ln_b, wa, wb, ga, gb, *, seq, tm, tn):
    t, d = glu.shape
    assert seq % tm == 0 and tm % HALO == 0 and cv_w.shape[0] - 1 <= HALO
    hb = tm // HALO
    row = pl.BlockSpec((tm, d), lambda i, j: (i, 0))
    vec = pl.BlockSpec((1, d), lambda i, j: (0, 0))
    wcol = pl.BlockSpec((d, d), lambda i, j: (0, 0), pipeline_mode=pl.Buffered(1))
    tile = pl.BlockSpec((tm, tn), lambda i, j: (i, j))
    return pl.pallas_call(
        functools.partial(_mix_kernel, blocks_per_seq=seq // tm),
        grid=(t // tm, d // tn),
        in_specs=[row, row,
                  pl.BlockSpec((HALO, d), lambda i, j: (jnp.maximum(i * hb - 1, 0), 0)),
                  pl.BlockSpec(cv_w.shape, lambda i, j: (0, 0)),
                  vec, vec, vec, wcol, wcol, tile, tile],
        out_specs=tile,
        out_shape=jax.ShapeDtypeStruct((t, d), BF16),
        scratch_shapes=[pltpu.VMEM((HALO + tm + SUBLANES, d), F32), pltpu.VMEM((tm, d), F32),
                        pltpu.VMEM((tm, d), BF16)],
        compiler_params=_params("parallel", "arbitrary"),
        name="branch_mix",
    )(sg, glu, glu, cv_w, cv_b, cv_ln_g, cv_ln_b, wa, wb, ga, gb)


def _oproj_kernel(m_ref, x_ref, wo_ref, g_ref, b_ref, x1_ref, x1b_ref, x1t_ref, *, alpha):
    mix = _dot(m_ref[...], wo_ref[...])
    x1 = _layer_norm(alpha * x_ref[...] + mix, g_ref[...], b_ref[...])
    x1_ref[...] = x1
    x1b_ref[...] = x1.astype(BF16)
    x1t_ref[...] = x1.T.astype(BF16)


def _oproj(m, x, wo, g, b, *, alpha, tm):
    t, d = x.shape
    row = pl.BlockSpec((tm, d), lambda i: (i, 0))
    vec = pl.BlockSpec((1, d), lambda i: (0, 0))
    return pl.pallas_call(
        functools.partial(_oproj_kernel, alpha=alpha),
        grid=(t // tm,),
        in_specs=[row, row, pl.BlockSpec((d, d), lambda i: (0, 0)), vec, vec],
        out_specs=[row, row, pl.BlockSpec((d, tm), lambda i: (0, i))],
        out_shape=[jax.ShapeDtypeStruct((t, d), F32), jax.ShapeDtypeStruct((t, d), BF16),
                   jax.ShapeDtypeStruct((d, t), BF16)],
        compiler_params=_params("parallel"),
        name="out_proj_ln1",
    )(m, x, wo, g, b)


def _top_k_rows(ss, k, tie_break):
    n = ss[0].shape[0]
    idx = lax.broadcasted_iota(jnp.int32, ss[0].shape, 0).astype(F32)
    ranks = [jnp.full(s.shape, float(k), F32) for s in ss]
    works = list(ss)
    tops = [[] for _ in ss]
    for r in range(k):
        for i in range(len(ss)):
            m = jnp.max(works[i], axis=0, keepdims=True)
            sel = works[i] == m
            if tie_break:
                first = jnp.min(jnp.where(sel, idx, float(n)), axis=0, keepdims=True)
                sel = idx == first
            ranks[i] = jnp.where(sel, float(r), ranks[i])
            works[i] = jnp.where(sel, -jnp.inf, works[i])
            tops[i].append(m)
    exact = None
    for rank in ranks:
        picked = jnp.sum((rank < float(k)).astype(F32), axis=0, keepdims=True)
        ok = jnp.all(picked == float(k))
        exact = ok if exact is None else jnp.logical_and(exact, ok)
    return tuple(ranks), tuple(jnp.concatenate(t, axis=0) for t in tops), exact


def _merge_top_k(t1, t2, k):
    aidx = lax.broadcasted_iota(jnp.int32, t1.shape, 0).astype(F32)
    nb = jnp.zeros(t1.shape, F32)
    head = t1 + t2[0:1]
    top = head[0:1]
    z = jnp.zeros_like(top)
    for _ in range(k):
        m = jnp.max(head, axis=0, keepdims=True)
        first = jnp.min(jnp.where(head == m, aidx, float(k)), axis=0, keepdims=True)
        sel = aidx == first
        z = z + jnp.exp(m - top)
        nb = jnp.where(sel, nb + 1.0, nb)
        nxt = jnp.sum(jnp.where(sel, nb, 0.0), axis=0, keepdims=True)
        t2n = jnp.sum(jnp.where(aidx == nxt, t2, 0.0), axis=0, keepdims=True)
        t2n = jnp.where(nxt >= float(k), -jnp.inf, t2n)
        head = jnp.where(sel, t1 + t2n, head)
    return nb, z


def _route_kernel(x_ref, wq0_ref, wqn_ref, k1_ref, k2_ref, u_ref, v_ref,
                  r2_ref, b2_ref, nb1_ref, a1_ref, ub_ref, vt_ref, q_ref, exact_ref):
    ub_ref[...] = u_ref[...].astype(BF16)
    vt_ref[...] = v_ref[...].T.astype(BF16)
    tc = x_ref.shape[0]
    half = k1_ref.shape[-1]
    k = PEER_TOPK
    h = pl.program_id(1)
    slot = h % 2

    @pl.when(h == 0)
    def _():
        q_ref[0] = _dot(x_ref[...], wq0_ref[...]).astype(BF16)

    k1 = k1_ref[0, 0].astype(BF16)
    k2 = k2_ref[0, 0].astype(BF16)
    nt = (((1,), (1,)), ((), ()))

    def chunk(c, tie_break):
        start = c * ROUTE_CHUNK
        ts = pl.ds(start if isinstance(c, int) else pl.multiple_of(start, ROUTE_CHUNK), ROUTE_CHUNK)
        qc = q_ref[slot, ts, :]
        s1 = lax.dot_general(k1, qc[:, :half], nt, preferred_element_type=F32)
        s2 = lax.dot_general(k2, qc[:, half:], nt, preferred_element_type=F32)
        (rank1, rank2), (t1, t2), exact = _top_k_rows((s1, s2), k, tie_break)
        nb, z = _merge_top_k(t1, t2, k)
        rank1 = rank1.astype(BF16)
        nb = nb.astype(BF16)
        nb1 = jnp.zeros(rank1.shape, BF16)
        for a in range(k):
            nb1 = jnp.where(rank1 == a, nb[a:a + 1], nb1)
        r2_ref[0, :, ts] = rank2.astype(BF16)
        b2_ref[0, :, ts] = jnp.exp(s2 - t2[0:1]).astype(BF16)
        nb1_ref[0, :, ts] = nb1.astype(F32)
        a1_ref[0, :, ts] = jnp.exp(s1 - t1[0:1]) * (0.5 / z)
        return exact

    for c in range(tc // ROUTE_CHUNK):
        exact_ref[c] = chunk(c, False).astype(jnp.int32)
    q_ref[1 - slot] = _dot(x_ref[...], wqn_ref[...]).astype(BF16)

    def redo(c, carry):
        @pl.when(exact_ref[c] == 0)
        def _():
            chunk(c, True)
        return carry

    lax.fori_loop(0, tc // ROUTE_CHUNK, redo, 0)


def _route(x1b, wq, keys, u, v, *, tc):
    t, d = x1b.shape
    heads, _, nkeys, half = keys.shape
    nexp = u.shape[0]
    steps = (t // tc) * heads
    assert nexp % steps == 0 and (nexp // steps) % LANES == 0
    er = nexp // steps
    out = pl.BlockSpec((1, nkeys, tc), lambda i, h: (h, 0, i))
    shape = jax.ShapeDtypeStruct((heads, nkeys, t), F32)
    shape_bf16 = jax.ShapeDtypeStruct((heads, nkeys, t), BF16)
    rows = pl.BlockSpec((er, d), lambda i, h: (i * heads + h, 0))
    return pl.pallas_call(
        _route_kernel,
        grid=(t // tc, heads),
        in_specs=[pl.BlockSpec((tc, d), lambda i, h: (i, 0)),
                  pl.BlockSpec((d, 2 * half), lambda i, h: (0, 0)),
                  pl.BlockSpec((d, 2 * half), lambda i, h: (0, jnp.minimum(h + 1, heads - 1))),
                  pl.BlockSpec((1, 1, nkeys, half), lambda i, h: (h, 0, 0, 0)),
                  pl.BlockSpec((1, 1, nkeys, half), lambda i, h: (h, 1, 0, 0)),
                  rows, rows],
        out_specs=[out, out, out, out, rows, pl.BlockSpec((d, er), lambda i, h: (0, i * heads + h))],
        out_shape=[shape_bf16, shape_bf16, shape, shape,
                   jax.ShapeDtypeStruct((nexp, d), BF16), jax.ShapeDtypeStruct((d, nexp), BF16)],
        scratch_shapes=[pltpu.VMEM((2, tc, 2 * half), BF16), pltpu.SMEM((tc // ROUTE_CHUNK,), jnp.int32)],
        compiler_params=_params("parallel", "arbitrary"),
        name="peer_route",
    )(x1b, wq, wq, keys, keys, u, v)


def _experts_kernel(xt_ref, u_ref, vt_ref, r2_ref, b2_ref, nb1_ref, a1_ref, y_ref, acc_ref, *, rows_per_step):
    s = pl.program_id(1)
    heads = r2_ref.shape[0]

    @pl.when(s == 0)
    def _():
        acc_ref[...] = jnp.zeros_like(acc_ref)

    nkeys = r2_ref.shape[1]
    sub = EXPERT_SUB_ROWS
    acts = []
    for q in range(rows_per_step // sub):
        wts = []
        for r in range(q * sub, (q + 1) * sub):
            i1 = s * rows_per_step + r
            wt = None
            for hd in range(heads):
                nb_row = nb1_ref[hd, pl.ds(i1, 1), :].astype(BF16)
                a_row = a1_ref[hd, pl.ds(i1, 1), :].astype(BF16)
                zero = jnp.zeros((), BF16)
                term = jnp.where(r2_ref[hd] < nb_row, b2_ref[hd], zero) * a_row
                wt = term if wt is None else wt + term
            wts.append(wt)
        w = jnp.concatenate(wts, axis=0)
        ht = _dot(u_ref[q * sub * nkeys:(q + 1) * sub * nkeys, :], xt_ref[...])
        two_gelu = ht * (1.0 + lax.erf(ht * SQRT_HALF))
        acts.append(w * two_gelu.astype(BF16))
    act = jnp.concatenate(acts, axis=0)
    acc_ref[...] += _dot(vt_ref[...], act)

    @pl.when(s == pl.num_programs(1) - 1)
    def _():
        y_ref[...] = acc_ref[...].T


def _experts(x1t, ub, vt, r2, b2, nb1, a1, *, tm, rows_per_step):
    d, t = x1t.shape
    heads, nkeys, _ = r2.shape
    eb = rows_per_step * nkeys
    route = pl.BlockSpec((heads, nkeys, tm), lambda i, s: (0, 0, i))
    return pl.pallas_call(
        functools.partial(_experts_kernel, rows_per_step=rows_per_step),
        grid=(t // tm, nkeys // rows_per_step),
        in_specs=[pl.BlockSpec((d, tm), lambda i, s: (0, i)),
                  pl.BlockSpec((eb, d), lambda i, s: (s, 0)),
                  pl.BlockSpec((d, eb), lambda i, s: (0, s)),
                  route, route, route, route],
        out_specs=pl.BlockSpec((tm, d), lambda i, s: (i, 0)),
        out_shape=jax.ShapeDtypeStruct((t, d), F32),
        scratch_shapes=[pltpu.VMEM((d, tm), F32)],
        compiler_params=_params("parallel", "arbitrary"),
        name="peer_experts",
    )(x1t, ub, vt, r2, b2, nb1, a1)


def _tail_kernel(x1_ref, y_ref, p_ref, g_ref, b_ref, wg_ref, wp_ref, o_ref, *, alpha):
    x2 = _layer_norm(alpha * x1_ref[...] + y_ref[...], g_ref[...], b_ref[...])
    gate = _sigmoid(_dot(x2.astype(BF16), wg_ref[...]))
    proj = _dot(p_ref[...].astype(BF16), wp_ref[...])
    o_ref[...] = x2 + gate * proj


def _tail(x1, y, p, g, b, wg, wp, *, alpha, tm):
    t, d = x1.shape
    pd = p.shape[1]
    row = pl.BlockSpec((tm, d), lambda i: (i, 0))
    vec = pl.BlockSpec((1, d), lambda i: (0, 0))
    return pl.pallas_call(
        functools.partial(_tail_kernel, alpha=alpha),
        grid=(t // tm,),
        in_specs=[row, row, pl.BlockSpec((tm, pd), lambda i: (i, 0)), vec, vec,
                  pl.BlockSpec((d, d), lambda i: (0, 0)), pl.BlockSpec((pd, d), lambda i: (0, 0))],
        out_specs=row,
        out_shape=jax.ShapeDtypeStruct((t, d), F32),
        compiler_params=_params("parallel"),
        name="ln2_ple",
    )(x1, y, p, g, b, wg, wp)


def kernel(x, p, w_in, b_in, gm_ln_g, gm_ln_b, gm_ws, gm_bs, w_gm_out, cv_w, cv_b, cv_ln_g, cv_ln_b,
           w_cv_out, w_o, ln1_g, ln1_b, peer_wq, peer_keys, peer_u, peer_v, ln2_g, ln2_b,
           ple_w_gate, ple_w_proj):
    bsz, seq, d = x.shape
    depth = w_in.shape[0]
    t = bsz * seq
    alpha = (2.0 * depth) ** 0.25
    groups, blk, _ = gm_ws.shape[1:]
    gdim = d // groups
    tm = min(512, seq)

    xf = x.reshape(t, d)
    for i in range(depth):
        row = lambda a: a[i].reshape(1, -1)
        bs_b = jnp.broadcast_to(gm_bs[i][:, :, None], (groups, blk, gdim))
        (sg, glu, ga, gb), (w_a, w_b, w_ob, wq_b, wg_b) = _inproj(
            xf, w_in[i].astype(BF16), row(b_in), row(gm_ln_g), row(gm_ln_b), gm_ws[i], bs_b,
            [w_gm_out[i], w_cv_out[i], w_o[i], peer_wq[i], ple_w_gate[i]], tm=min(1024, t), tn=gdim)
        m = _mix(sg, glu, cv_w[i], row(cv_b), row(cv_ln_g), row(cv_ln_b), w_a, w_b, ga, gb,
                 seq=seq, tm=tm, tn=1024)
        x1, x1b, x1t = _oproj(m, xf, w_ob, row(ln1_g), row(ln1_b), alpha=alpha, tm=tm)
        r2, b2, nb1, a1, ub, vt = _route(x1b, wq_b, peer_keys[i], peer_u[i], peer_v[i], tc=min(1024, t))
        y = _experts(x1t, ub, vt, r2, b2, nb1, a1, tm=tm, rows_per_step=8)
        xf = _tail(x1, y, p[i].reshape(t, -1), row(ln2_g), row(ln2_b),
                   wg_b, ple_w_proj[i].astype(BF16), alpha=alpha, tm=tm)
    return xf.reshape(bsz, seq, d)
```

```python
import functools

import jax
import jax.numpy as jnp
from jax import lax
from jax.experimental import pallas as pl
from jax.experimental.pallas import tpu as pltpu

LN_EPS = 1e-5
CHUNK = 64
PEER_TOPK = 16
SQRT_HALF = 0.7071067811865476
LANES = 128
SUBLANES = 8
EXPERT_SUB_ROWS = 2
ROUTE_CHUNK = 256
HALO = 32
VMEM_LIMIT = 56 * 1024 * 1024

F32 = jnp.float32
BF16 = jnp.bfloat16


def _gelu(x):
    return 0.5 * x * (1.0 + lax.erf(x * SQRT_HALF))


def _sigmoid(x):
    return 1.0 / (1.0 + jnp.exp(-x))


def _layer_norm(x, g, b):
    mu = jnp.mean(x, axis=-1, keepdims=True)
    d = x - mu
    var = jnp.mean(d * d, axis=-1, keepdims=True)
    return d * lax.rsqrt(var + LN_EPS) * g + b


def _dot(a, b):
    return jnp.dot(a, b, preferred_element_type=F32)


def _params(*sem):
    return pltpu.CompilerParams(dimension_semantics=sem, vmem_limit_bytes=VMEM_LIMIT)


N_INPROJ_INPUTS = 10


def _inproj_kernel(*refs, n_cast):
    (x_ref, wu_ref, wv_ref, wa_ref, wb_ref, wga_ref, wgb_ref, vec_ref, ws_ref, bs_ref) = refs[:N_INPROJ_INPUTS]
    bu_ref, bv_ref, ba_ref, bb_ref, bga_ref, bgb_ref, lng_ref, lnb_ref = (
        vec_ref.at[r:r + 1, :] for r in range(8))
    cast_in = refs[N_INPROJ_INPUTS:N_INPROJ_INPUTS + n_cast]
    sg_ref, glu_ref, ga_ref, gb_ref = refs[N_INPROJ_INPUTS + n_cast:N_INPROJ_INPUTS + n_cast + 4]
    cast_out = refs[N_INPROJ_INPUTS + n_cast + 4:N_INPROJ_INPUTS + 2 * n_cast + 4]
    xb_ref = refs[-1]

    for src, dst in zip(cast_in, cast_out):
        dst[...] = src[...].astype(BF16)

    @pl.when(pl.program_id(1) == 0)
    def _():
        xb_ref[...] = x_ref[...].astype(BF16)

    x = xb_ref[...]
    tm = x.shape[0]
    blk = ws_ref.shape[1]

    def proj(w_ref, b_ref):
        return _dot(x, w_ref[...]) + b_ref[...]

    u = _gelu(proj(wu_ref, bu_ref))
    v = _gelu(proj(wv_ref, bv_ref))
    vn = _layer_norm(v, lng_ref[...], lnb_ref[...]).astype(BF16)
    row = lax.broadcasted_iota(jnp.int32, (blk, blk), 0)
    col = lax.broadcasted_iota(jnp.int32, (blk, blk), 1)
    w = jnp.where(col // CHUNK <= row // CHUNK, ws_ref[0], 0.0).astype(BF16)
    bs = bs_ref[0]
    for nb in range(tm // blk):
        rows = slice(nb * blk, (nb + 1) * blk)
        mixed = _dot(w, vn[rows]) + bs
        sg_ref[rows, :] = (u[rows] * mixed).astype(BF16)

    glu_ref[...] = proj(wa_ref, ba_ref) * _sigmoid(proj(wb_ref, bb_ref))
    ga_ref[...] = _sigmoid(proj(wga_ref, bga_ref)).astype(BF16)
    gb_ref[...] = _sigmoid(proj(wgb_ref, bgb_ref)).astype(BF16)


def _inproj(x, w_in, b_in, ln_g, ln_b, ws, bs_b, later_weights, *, tm, tn):
    t, d = x.shape
    nseg = w_in.shape[1] // d
    ncol = d // tn
    steps = (t // tm) * ncol
    cast_specs, cast_shapes = [], []
    for w in later_weights:
        assert w.shape[0] % steps == 0 and (w.shape[0] // steps) % (2 * SUBLANES) == 0
        cast_specs.append(pl.BlockSpec((w.shape[0] // steps, w.shape[1]), lambda i, j: (i * ncol + j, 0)))
        cast_shapes.append(jax.ShapeDtypeStruct(w.shape, BF16))
    w_specs = [pl.BlockSpec((d, tn), functools.partial(lambda i, j, s: (0, s * ncol + j), s=s))
               for s in range(nseg)]
    vecs = jnp.concatenate([b_in.reshape(nseg, d), ln_g, ln_b], axis=0)
    tile = pl.BlockSpec((tm, tn), lambda i, j: (i, j))
    in_specs = [pl.BlockSpec((tm, d), lambda i, j: (i, 0))] + w_specs + [
        pl.BlockSpec((vecs.shape[0], tn), lambda i, j: (0, j)),
        pl.BlockSpec((1,) + ws.shape[1:], lambda i, j: (j, 0, 0)),
        pl.BlockSpec((1,) + bs_b.shape[1:], lambda i, j: (j, 0, 0)),
    ]
    assert len(in_specs) == N_INPROJ_INPUTS
    outs = pl.pallas_call(
        functools.partial(_inproj_kernel, n_cast=len(later_weights)),
        grid=(t // tm, ncol),
        in_specs=in_specs + cast_specs,
        out_specs=[tile, tile, tile, tile] + cast_specs,
        out_shape=[jax.ShapeDtypeStruct((t, d), BF16), jax.ShapeDtypeStruct((t, d), F32),
                   jax.ShapeDtypeStruct((t, d), BF16), jax.ShapeDtypeStruct((t, d), BF16)] + cast_shapes,
        scratch_shapes=[pltpu.VMEM((tm, d), BF16)],
        compiler_params=_params("parallel", "arbitrary"),
        name="inproj",
    )(x, *([w_in] * nseg), vecs, ws, bs_b, *later_weights)
    return outs[:4], outs[4:]


def _mix_kernel(sg_ref, glu_ref, halo_ref, cw_ref, cb_ref, clg_ref, clb_ref,
                wa_ref, wb_ref, ga_ref, gb_ref, m_ref, buf_ref, c_ref, cn_ref, *, blocks_per_seq):
    i = pl.program_id(0)
    j = pl.program_id(1)
    tm, d = glu_ref.shape
    taps = cw_ref.shape[0]

    @pl.when(j == 0)
    def _():
        keep = (i % blocks_per_seq != 0).astype(F32)
        buf_ref[0:HALO, :] = halo_ref[...] * keep
        buf_ref[HALO:HALO + tm, :] = glu_ref[...]
        buf_ref[HALO + tm:, :] = jnp.zeros((SUBLANES, d), F32)
        base = HALO - (taps - 1)

        def cols(c, carry):
            cs = pl.ds(pl.multiple_of(c * LANES, LANES), LANES)
            acc = jnp.broadcast_to(cb_ref[:, cs], (tm, LANES))
            for sh in range(SUBLANES):
                part = None
                for k in range(taps):
                    off = base + k
                    if off % SUBLANES == sh:
                        term = cw_ref[k:k + 1, cs] * buf_ref[off - sh:off - sh + tm + SUBLANES, cs]
                        part = term if part is None else part + term
                if part is not None:
                    acc = acc + part[sh:sh + tm]
            c_ref[:, cs] = acc
            return carry

        lax.fori_loop(0, d // LANES, cols, 0)
        c = _layer_norm(c_ref[...], clg_ref[...], clb_ref[...])
        cn_ref[...] = (c * _sigmoid(c)).astype(BF16)

    tn = m_ref.shape[1]
    cols = pl.ds(pl.multiple_of(j * tn, tn), tn)
    ya = _dot(sg_ref[...], wa_ref[:, cols])
    yb = _dot(cn_ref[...], wb_ref[:, cols])
    m_ref[...] = (ga_ref[...].astype(F32) * ya + gb_ref[...].astype(F32) * yb).astype(BF16)


def _mix(sg, glu, cv_w, cv_b, cv_ln_g, cv_ln_b, wa, wb, ga, gb, *, seq, tm, tn):
    t, d = glu.shape
    assert seq % tm == 0 and tm % HALO == 0 and cv_w.shape[0] - 1 <= HALO
    hb = tm // HALO
    row = pl.BlockSpec((tm, d), lambda i, j: (i, 0))
    vec = pl.BlockSpec((1, d), lambda i, j: (0, 0))
    wcol = pl.BlockSpec((d, d), lambda i, j: (0, 0), pipeline_mode=pl.Buffered(1))
    tile = pl.BlockSpec((tm, tn), lambda i, j: (i, j))
    return pl.pallas_call(
        functools.partial(_mix_kernel, blocks_per_seq=seq // tm),
        grid=(t // tm, d // tn),
        in_specs=[row, row,
                  pl.BlockSpec((HALO, d), lambda i, j: (jnp.maximum(i * hb - 1, 0), 0)),
                  pl.BlockSpec(cv_w.shape, lambda i, j: (0, 0)),
                  vec, vec, vec, wcol, wcol, tile, tile],
        out_specs=tile,
        out_shape=jax.ShapeDtypeStruct((t, d), BF16),
        scratch_shapes=[pltpu.VMEM((HALO + tm + SUBLANES, d), F32), pltpu.VMEM((tm, d), F32),
                        pltpu.VMEM((tm, d), BF16)],
        compiler_params=_params("parallel", "arbitrary"),
        name="branch_mix",
    )(sg, glu, glu, cv_w, cv_b, cv_ln_g, cv_ln_b, wa, wb, ga, gb)


def _oproj_kernel(m_ref, x_ref, wo_ref, g_ref, b_ref, x1_ref, x1b_ref, x1t_ref, *, alpha):
    mix = _dot(m_ref[...], wo_ref[...])
    x1 = _layer_norm(alpha * x_ref[...] + mix, g_ref[...], b_ref[...])
    x1_ref[...] = x1
    x1b_ref[...] = x1.astype(BF16)
    x1t_ref[...] = x1.T.astype(BF16)


def _oproj(m, x, wo, g, b, *, alpha, tm):
    t, d = x.shape
    row = pl.BlockSpec((tm, d), lambda i: (i, 0))
    vec = pl.BlockSpec((1, d), lambda i: (0, 0))
    return pl.pallas_call(
        functools.partial(_oproj_kernel, alpha=alpha),
        grid=(t // tm,),
        in_specs=[row, row, pl.BlockSpec((d, d), lambda i: (0, 0)), vec, vec],
        out_specs=[row, row, pl.BlockSpec((d, tm), lambda i: (0, i))],
        out_shape=[jax.ShapeDtypeStruct((t, d), F32), jax.ShapeDtypeStruct((t, d), BF16),
                   jax.ShapeDtypeStruct((d, t), BF16)],
        compiler_params=_params("parallel"),
        name="out_proj_ln1",
    )(m, x, wo, g, b)


def _top_k_rows(ss, k, tie_break):
    n = ss[0].shape[0]
    idx = lax.broadcasted_iota(jnp.int32, ss[0].shape, 0).astype(F32)
    ranks = [jnp.full(s.shape, float(k), F32) for s in ss]
    works = list(ss)
    tops = [[] for _ in ss]
    for r in range(k):
        for i in range(len(ss)):
            m = jnp.max(works[i], axis=0, keepdims=True)
            sel = works[i] == m
            if tie_break:
                first = jnp.min(jnp.where(sel, idx, float(n)), axis=0, keepdims=True)
                sel = idx == first
            ranks[i] = jnp.where(sel, float(r), ranks[i])
            works[i] = jnp.where(sel, -jnp.inf, works[i])
            tops[i].append(m)
    exact = None
    for rank in ranks:
        picked = jnp.sum((rank < float(k)).astype(F32), axis=0, keepdims=True)
        ok = jnp.all(picked == float(k))
        exact = ok if exact is None else jnp.logical_and(exact, ok)
    return tuple(ranks), tuple(jnp.concatenate(t, axis=0) for t in tops), exact


def _merge_top_k(t1, t2, k):
    aidx = lax.broadcasted_iota(jnp.int32, t1.shape, 0).astype(F32)
    nb = jnp.zeros(t1.shape, F32)
    head = t1 + t2[0:1]
    top = head[0:1]
    z = jnp.zeros_like(top)
    for _ in range(k):
        m = jnp.max(head, axis=0, keepdims=True)
        first = jnp.min(jnp.where(head == m, aidx, float(k)), axis=0, keepdims=True)
        sel = aidx == first
        z = z + jnp.exp(m - top)
        nb = jnp.where(sel, nb + 1.0, nb)
        nxt = jnp.sum(jnp.where(sel, nb, 0.0), axis=0, keepdims=True)
        t2n = jnp.sum(jnp.where(aidx == nxt, t2, 0.0), axis=0, keepdims=True)
        t2n = jnp.where(nxt >= float(k), -jnp.inf, t2n)
        head = jnp.where(sel, t1 + t2n, head)
    return nb, z


def _route_kernel(x_ref, wq0_ref, wqn_ref, k1_ref, k2_ref, u_ref, v_ref,
                  r2_ref, b2_ref, nb1_ref, a1_ref, ub_ref, vt_ref, q_ref, exact_ref):
    ub_ref[...] = u_ref[...].astype(BF16)
    vt_ref[...] = v_ref[...].T.astype(BF16)
    tc = x_ref.shape[0]
    hp, _, _, half = k1_ref.shape
    n_chunks = tc // ROUTE_CHUNK
    k = PEER_TOPK
    g = pl.program_id(1)
    slot = g % 2

    @pl.when(g == 0)
    def _():
        q_ref[0] = _dot(x_ref[...], wq0_ref[...]).astype(BF16)

    nt = (((1,), (1,)), ((), ()))

    def chunk(hh, c, tie_break):
        static = isinstance(c, int)
        start = c * ROUTE_CHUNK
        ts = pl.ds(start if static else pl.multiple_of(start, ROUTE_CHUNK), ROUTE_CHUNK)
        qs = pl.ds(hh * 2 * half if static else pl.multiple_of(hh * 2 * half, 2 * half), 2 * half)
        qc = q_ref[slot, ts, qs]
        k1 = k1_ref[hh, 0].astype(BF16)
        k2 = k2_ref[hh, 0].astype(BF16)
        s1 = lax.dot_general(k1, qc[:, :half], nt, preferred_element_type=F32)
        s2 = lax.dot_general(k2, qc[:, half:], nt, preferred_element_type=F32)
        (rank1, rank2), (t1, t2), exact = _top_k_rows((s1, s2), k, tie_break)
        nb, z = _merge_top_k(t1, t2, k)
        rank1 = rank1.astype(BF16)
        nb = nb.astype(BF16)
        nb1 = jnp.zeros(rank1.shape, BF16)
        for a in range(k):
            nb1 = jnp.where(rank1 == a, nb[a:a + 1], nb1)
        r2_ref[hh, :, ts] = rank2.astype(BF16)
        b2_ref[hh, :, ts] = jnp.exp(s2 - t2[0:1]).astype(BF16)
        nb1_ref[hh, :, ts] = nb1.astype(F32)
        a1_ref[hh, :, ts] = jnp.exp(s1 - t1[0:1]) * (0.5 / z)
        return exact

    for hh in range(hp):
        for c in range(n_chunks):
            exact_ref[hh * n_chunks + c] = chunk(hh, c, False).astype(jnp.int32)
    q_ref[1 - slot] = _dot(x_ref[...], wqn_ref[...]).astype(BF16)

    def redo(p, carry):
        @pl.when(exact_ref[p] == 0)
        def _():
            chunk(p // n_chunks, p % n_chunks, True)
        return carry

    lax.fori_loop(0, hp * n_chunks, redo, 0)


def _route(x1b, wq, keys, u, v, *, tc, hp):
    t, d = x1b.shape
    heads, _, nkeys, half = keys.shape
    nexp = u.shape[0]
    groups = heads // hp
    steps = (t // tc) * groups
    assert heads % hp == 0 and nexp % steps == 0 and (nexp // steps) % LANES == 0
    er = nexp // steps
    out = pl.BlockSpec((hp, nkeys, tc), lambda i, g: (g, 0, i))
    shape = jax.ShapeDtypeStruct((heads, nkeys, t), F32)
    shape_bf16 = jax.ShapeDtypeStruct((heads, nkeys, t), BF16)
    rows = pl.BlockSpec((er, d), lambda i, g: (i * groups + g, 0))
    qw = hp * 2 * half
    return pl.pallas_call(
        _route_kernel,
        grid=(t // tc, groups),
        in_specs=[pl.BlockSpec((tc, d), lambda i, g: (i, 0)),
                  pl.BlockSpec((d, qw), lambda i, g: (0, 0)),
                  pl.BlockSpec((d, qw), lambda i, g: (0, jnp.minimum(g + 1, groups - 1))),
                  pl.BlockSpec((hp, 1, nkeys, half), lambda i, g: (g, 0, 0, 0)),
                  pl.BlockSpec((hp, 1, nkeys, half), lambda i, g: (g, 1, 0, 0)),
                  rows, rows],
        out_specs=[out, out, out, out, rows, pl.BlockSpec((d, er), lambda i, g: (0, i * groups + g))],
        out_shape=[shape_bf16, shape_bf16, shape, shape,
                   jax.ShapeDtypeStruct((nexp, d), BF16), jax.ShapeDtypeStruct((d, nexp), BF16)],
        scratch_shapes=[pltpu.VMEM((2, tc, qw), BF16), pltpu.SMEM((hp * (tc // ROUTE_CHUNK),), jnp.int32)],
        compiler_params=_params("parallel", "arbitrary"),
        name="peer_route",
    )(x1b, wq, wq, keys, keys, u, v)


def _experts_kernel(xt_ref, u_ref, vt_ref, r2_ref, b2_ref, nb1_ref, a1_ref, y_ref, acc_ref, *, rows_per_step):
    s = pl.program_id(1)
    heads = r2_ref.shape[0]

    @pl.when(s == 0)
    def _():
        acc_ref[...] = jnp.zeros_like(acc_ref)

    nkeys = r2_ref.shape[1]
    sub = EXPERT_SUB_ROWS
    acts = []
    for q in range(rows_per_step // sub):
        wts = []
        for r in range(q * sub, (q + 1) * sub):
            i1 = s * rows_per_step + r
            wt = None
            for hd in range(heads):
                nb_row = nb1_ref[hd, pl.ds(i1, 1), :].astype(BF16)
                a_row = a1_ref[hd, pl.ds(i1, 1), :].astype(BF16)
                zero = jnp.zeros((), BF16)
                term = jnp.where(r2_ref[hd] < nb_row, b2_ref[hd], zero) * a_row
                wt = term if wt is None else wt + term
            wts.append(wt)
        w = jnp.concatenate(wts, axis=0)
        ht = _dot(u_ref[q * sub * nkeys:(q + 1) * sub * nkeys, :], xt_ref[...])
        two_gelu = ht * (1.0 + lax.erf(ht * SQRT_HALF))
        acts.append(w * two_gelu.astype(BF16))
    act = jnp.concatenate(acts, axis=0)
    acc_ref[...] += _dot(vt_ref[...], act)

    @pl.when(s == pl.num_programs(1) - 1)
    def _():
        y_ref[...] = acc_ref[...].T


def _experts(x1t, ub, vt, r2, b2, nb1, a1, *, tm, rows_per_step):
    d, t = x1t.shape
    heads, nkeys, _ = r2.shape
    eb = rows_per_step * nkeys
    route = pl.BlockSpec((heads, nkeys, tm), lambda i, s: (0, 0, i))
    return pl.pallas_call(
        functools.partial(_experts_kernel, rows_per_step=rows_per_step),
        grid=(t // tm, nkeys // rows_per_step),
        in_specs=[pl.BlockSpec((d, tm), lambda i, s: (0, i)),
                  pl.BlockSpec((eb, d), lambda i, s: (s, 0)),
                  pl.BlockSpec((d, eb), lambda i, s: (0, s)),
                  route, route, route, route],
        out_specs=pl.BlockSpec((tm, d), lambda i, s: (i, 0)),
        out_shape=jax.ShapeDtypeStruct((t, d), F32),
        scratch_shapes=[pltpu.VMEM((d, tm), F32)],
        compiler_params=_params("parallel", "arbitrary"),
        name="peer_experts",
    )(x1t, ub, vt, r2, b2, nb1, a1)


def _tail_kernel(x1_ref, y_ref, p_ref, g_ref, b_ref, wg_ref, wp_ref, o_ref, *, alpha):
    x2 = _layer_norm(alpha * x1_ref[...] + y_ref[...], g_ref[...], b_ref[...])
    gate = _sigmoid(_dot(x2.astype(BF16), wg_ref[...]))
    proj = _dot(p_ref[...].astype(BF16), wp_ref[...])
    o_ref[...] = x2 + gate * proj


def _tail(x1, y, p, g, b, wg, wp, *, alpha, tm):
    t, d = x1.shape
    pd = p.shape[1]
    row = pl.BlockSpec((tm, d), lambda i: (i, 0))
    vec = pl.BlockSpec((1, d), lambda i: (0, 0))
    return pl.pallas_call(
        functools.partial(_tail_kernel, alpha=alpha),
        grid=(t // tm,),
        in_specs=[row, row, pl.BlockSpec((tm, pd), lambda i: (i, 0)), vec, vec,
                  pl.BlockSpec((d, d), lambda i: (0, 0)), pl.BlockSpec((pd, d), lambda i: (0, 0))],
        out_specs=row,
        out_shape=jax.ShapeDtypeStruct((t, d), F32),
        compiler_params=_params("parallel"),
        name="ln2_ple",
    )(x1, y, p, g, b, wg, wp)


def kernel(x, p, w_in, b_in, gm_ln_g, gm_ln_b, gm_ws, gm_bs, w_gm_out, cv_w, cv_b, cv_ln_g, cv_ln_b,
           w_cv_out, w_o, ln1_g, ln1_b, peer_wq, peer_keys, peer_u, peer_v, ln2_g, ln2_b,
           ple_w_gate, ple_w_proj):
    bsz, seq, d = x.shape
    depth = w_in.shape[0]
    t = bsz * seq
    alpha = (2.0 * depth) ** 0.25
    groups, blk, _ = gm_ws.shape[1:]
    gdim = d // groups
    tm = min(512, seq)

    xf = x.reshape(t, d)
    for i in range(depth):
        row = lambda a: a[i].reshape(1, -1)
        bs_b = jnp.broadcast_to(gm_bs[i][:, :, None], (groups, blk, gdim))
        (sg, glu, ga, gb), (w_a, w_b, w_ob, wq_b, wg_b) = _inproj(
            xf, w_in[i].astype(BF16), row(b_in), row(gm_ln_g), row(gm_ln_b), gm_ws[i], bs_b,
            [w_gm_out[i], w_cv_out[i], w_o[i], peer_wq[i], ple_w_gate[i]], tm=min(1024, t), tn=gdim)
        m = _mix(sg, glu, cv_w[i], row(cv_b), row(cv_ln_g), row(cv_ln_b), w_a, w_b, ga, gb,
                 seq=seq, tm=tm, tn=1024)
        x1, x1b, x1t = _oproj(m, xf, w_ob, row(ln1_g), row(ln1_b), alpha=alpha, tm=tm)
        r2, b2, nb1, a1, ub, vt = _route(x1b, wq_b, peer_keys[i], peer_u[i], peer_v[i], tc=min(1024, t), hp=2)
        y = _experts(x1t, ub, vt, r2, b2, nb1, a1, tm=tm, rows_per_step=8)
        xf = _tail(x1, y, p[i].reshape(t, -1), row(ln2_g), row(ln2_b),
                   wg_b, ple_w_proj[i].astype(BF16), alpha=alpha, tm=tm)
    return xf.reshape(bsz, seq, d)
```

```python
import functools

import jax
import jax.numpy as jnp
from jax import lax
from jax.experimental import pallas as pl
from jax.experimental.pallas import tpu as pltpu

LN_EPS = 1e-5
CHUNK = 64
PEER_TOPK = 16
SQRT_HALF = 0.7071067811865476
LANES = 128
SUBLANES = 8
EXPERT_SUB_ROWS = 2
ROUTE_CHUNK = 256
HALO = 32
VMEM_LIMIT = 56 * 1024 * 1024

F32 = jnp.float32
BF16 = jnp.bfloat16


def _gelu(x):
    return 0.5 * x * (1.0 + lax.erf(x * SQRT_HALF))


def _sigmoid(x):
    return 1.0 / (1.0 + jnp.exp(-x))


def _layer_norm(x, g, b):
    mu = jnp.mean(x, axis=-1, keepdims=True)
    d = x - mu
    var = jnp.mean(d * d, axis=-1, keepdims=True)
    return d * lax.rsqrt(var + LN_EPS) * g + b


def _dot(a, b):
    return jnp.dot(a, b, preferred_element_type=F32)


def _params(*sem):
    return pltpu.CompilerParams(dimension_semantics=sem, vmem_limit_bytes=VMEM_LIMIT)


N_INPROJ_INPUTS = 10


def _inproj_kernel(*refs, n_cast):
    (x_ref, wu_ref, wv_ref, wa_ref, wb_ref, wga_ref, wgb_ref, vec_ref, ws_ref, bs_ref) = refs[:N_INPROJ_INPUTS]
    bu_ref, bv_ref, ba_ref, bb_ref, bga_ref, bgb_ref, lng_ref, lnb_ref = (
        vec_ref.at[r:r + 1, :] for r in range(8))
    cast_in = refs[N_INPROJ_INPUTS:N_INPROJ_INPUTS + n_cast]
    sg_ref, glu_ref, ga_ref, gb_ref = refs[N_INPROJ_INPUTS + n_cast:N_INPROJ_INPUTS + n_cast + 4]
    cast_out = refs[N_INPROJ_INPUTS + n_cast + 4:N_INPROJ_INPUTS + 2 * n_cast + 4]
    xb_ref = refs[-1]

    for src, dst in zip(cast_in, cast_out):
        dst[...] = src[...].astype(BF16)

    @pl.when(pl.program_id(1) == 0)
    def _():
        xb_ref[...] = x_ref[...].astype(BF16)

    x = xb_ref[...]
    tm = x.shape[0]
    blk = ws_ref.shape[1]

    def proj(w_ref, b_ref):
        return _dot(x, w_ref[...]) + b_ref[...]

    u = _gelu(proj(wu_ref, bu_ref))
    v = _gelu(proj(wv_ref, bv_ref))
    vn = _layer_norm(v, lng_ref[...], lnb_ref[...]).astype(BF16)
    row = lax.broadcasted_iota(jnp.int32, (blk, blk), 0)
    col = lax.broadcasted_iota(jnp.int32, (blk, blk), 1)
    w = jnp.where(col // CHUNK <= row // CHUNK, ws_ref[0], 0.0).astype(BF16)
    bs = bs_ref[0]
    for nb in range(tm // blk):
        rows = slice(nb * blk, (nb + 1) * blk)
        mixed = _dot(w, vn[rows]) + bs
        sg_ref[rows, :] = (u[rows] * mixed).astype(BF16)

    glu_ref[...] = proj(wa_ref, ba_ref) * _sigmoid(proj(wb_ref, bb_ref))
    ga_ref[...] = _sigmoid(proj(wga_ref, bga_ref)).astype(BF16)
    gb_ref[...] = _sigmoid(proj(wgb_ref, bgb_ref)).astype(BF16)


def _inproj(x, w_in, b_in, ln_g, ln_b, ws, bs_b, later_weights, *, tm, tn):
    t, d = x.shape
    nseg = w_in.shape[1] // d
    ncol = d // tn
    steps = (t // tm) * ncol
    cast_specs, cast_shapes = [], []
    for w in later_weights:
        assert w.shape[0] % steps == 0 and (w.shape[0] // steps) % (2 * SUBLANES) == 0
        cast_specs.append(pl.BlockSpec((w.shape[0] // steps, w.shape[1]), lambda i, j: (i * ncol + j, 0)))
        cast_shapes.append(jax.ShapeDtypeStruct(w.shape, BF16))
    w_specs = [pl.BlockSpec((d, tn), functools.partial(lambda i, j, s: (0, s * ncol + j), s=s))
               for s in range(nseg)]
    vecs = jnp.concatenate([b_in.reshape(nseg, d), ln_g, ln_b], axis=0)
    tile = pl.BlockSpec((tm, tn), lambda i, j: (i, j))
    in_specs = [pl.BlockSpec((tm, d), lambda i, j: (i, 0))] + w_specs + [
        pl.BlockSpec((vecs.shape[0], tn), lambda i, j: (0, j)),
        pl.BlockSpec((1,) + ws.shape[1:], lambda i, j: (j, 0, 0)),
        pl.BlockSpec((1,) + bs_b.shape[1:], lambda i, j: (j, 0, 0)),
    ]
    assert len(in_specs) == N_INPROJ_INPUTS
    outs = pl.pallas_call(
        functools.partial(_inproj_kernel, n_cast=len(later_weights)),
        grid=(t // tm, ncol),
        in_specs=in_specs + cast_specs,
        out_specs=[tile, tile, tile, tile] + cast_specs,
        out_shape=[jax.ShapeDtypeStruct((t, d), BF16), jax.ShapeDtypeStruct((t, d), F32),
                   jax.ShapeDtypeStruct((t, d), BF16), jax.ShapeDtypeStruct((t, d), BF16)] + cast_shapes,
        scratch_shapes=[pltpu.VMEM((tm, d), BF16)],
        compiler_params=_params("parallel", "arbitrary"),
        name="inproj",
    )(x, *([w_in] * nseg), vecs, ws, bs_b, *later_weights)
    return outs[:4], outs[4:]


def _mix_kernel(sg_ref, glu_ref, halo_ref, cw_ref, cb_ref, clg_ref, clb_ref,
                wa_ref, wb_ref, ga_ref, gb_ref, m_ref, buf_ref, c_ref, cn_ref, *, blocks_per_seq):
    i = pl.program_id(0)
    j = pl.program_id(1)
    tm, d = glu_ref.shape
    taps = cw_ref.shape[0]

    @pl.when(j == 0)
    def _():
        keep = (i % blocks_per_seq != 0).astype(F32)
        buf_ref[0:HALO, :] = halo_ref[...] * keep
        buf_ref[HALO:HALO + tm, :] = glu_ref[...]
        buf_ref[HALO + tm:, :] = jnp.zeros((SUBLANES, d), F32)
        base = HALO - (taps - 1)

        def cols(c, carry):
            cs = pl.ds(pl.multiple_of(c * LANES, LANES), LANES)
            acc = jnp.broadcast_to(cb_ref[:, cs], (tm, LANES))
            for sh in range(SUBLANES):
                part = None
                for k in range(taps):
                    off = base + k
                    if off % SUBLANES == sh:
                        term = cw_ref[k:k + 1, cs] * buf_ref[off - sh:off - sh + tm + SUBLANES, cs]
                        part = term if part is None else part + term
                if part is not None:
                    acc = acc + part[sh:sh + tm]
            c_ref[:, cs] = acc
            return carry

        lax.fori_loop(0, d // LANES, cols, 0)
        c = _layer_norm(c_ref[...], clg_ref[...], clb_ref[...])
        cn_ref[...] = (c * _sigmoid(c)).astype(BF16)

    tn = m_ref.shape[1]
    cols = pl.ds(pl.multiple_of(j * tn, tn), tn)
    ya = _dot(sg_ref[...], wa_ref[:, cols])
    yb = _dot(cn_ref[...], wb_ref[:, cols])
    m_ref[...] = (ga_ref[...].astype(F32) * ya + gb_ref[...].astype(F32) * yb).astype(BF16)


def _mix(sg, glu, cv_w, cv_b, cv_ln_g, cv_ln_b, wa, wb, ga, gb, *, seq, tm, tn):
    t, d = glu.shape
    assert seq % tm == 0 and tm % HALO == 0 and cv_w.shape[0] - 1 <= HALO
    hb = tm // HALO
    row = pl.BlockSpec((tm, d), lambda i, j: (i, 0))
    vec = pl.BlockSpec((1, d), lambda i, j: (0, 0))
    wcol = pl.BlockSpec((d, d), lambda i, j: (0, 0), pipeline_mode=pl.Buffered(1))
    tile = pl.BlockSpec((tm, tn), lambda i, j: (i, j))
    return pl.pallas_call(
        functools.partial(_mix_kernel, blocks_per_seq=seq // tm),
        grid=(t // tm, d // tn),
        in_specs=[row, row,
                  pl.BlockSpec((HALO, d), lambda i, j: (jnp.maximum(i * hb - 1, 0), 0)),
                  pl.BlockSpec(cv_w.shape, lambda i, j: (0, 0)),
                  vec, vec, vec, wcol, wcol, tile, tile],
        out_specs=tile,
        out_shape=jax.ShapeDtypeStruct((t, d), BF16),
        scratch_shapes=[pltpu.VMEM((HALO + tm + SUBLANES, d), F32), pltpu.VMEM((tm, d), F32),
                        pltpu.VMEM((tm, d), BF16)],
        compiler_params=_params("parallel", "arbitrary"),
        name="branch_mix",
    )(sg, glu, glu, cv_w, cv_b, cv_ln_g, cv_ln_b, wa, wb, ga, gb)


def _oproj_kernel(m_ref, x_ref, wo_ref, g_ref, b_ref, x1_ref, x1b_ref, x1t_ref, *, alpha):
    mix = _dot(m_ref[...], wo_ref[...])
    x1 = _layer_norm(alpha * x_ref[...] + mix, g_ref[...], b_ref[...])
    x1_ref[...] = x1
    x1b_ref[...] = x1.astype(BF16)
    x1t_ref[...] = x1.T.astype(BF16)


def _oproj(m, x, wo, g, b, *, alpha, tm):
    t, d = x.shape
    row = pl.BlockSpec((tm, d), lambda i: (i, 0))
    vec = pl.BlockSpec((1, d), lambda i: (0, 0))
    return pl.pallas_call(
        functools.partial(_oproj_kernel, alpha=alpha),
        grid=(t // tm,),
        in_specs=[row, row, pl.BlockSpec((d, d), lambda i: (0, 0)), vec, vec],
        out_specs=[row, row, pl.BlockSpec((d, tm), lambda i: (0, i))],
        out_shape=[jax.ShapeDtypeStruct((t, d), F32), jax.ShapeDtypeStruct((t, d), BF16),
                   jax.ShapeDtypeStruct((d, t), BF16)],
        compiler_params=_params("parallel"),
        name="out_proj_ln1",
    )(m, x, wo, g, b)


def _top_k_rows(ss, k, tie_break):
    n = ss[0].shape[0]
    idx = lax.broadcasted_iota(jnp.int32, ss[0].shape, 0).astype(F32)
    ranks = [jnp.full(s.shape, float(k), F32) for s in ss]
    works = list(ss)
    tops = [[] for _ in ss]
    for r in range(k):
        for i in range(len(ss)):
            m = jnp.max(works[i], axis=0, keepdims=True)
            sel = works[i] == m
            if tie_break:
                first = jnp.min(jnp.where(sel, idx, float(n)), axis=0, keepdims=True)
                sel = idx == first
            ranks[i] = jnp.where(sel, float(r), ranks[i])
            works[i] = jnp.where(sel, -jnp.inf, works[i])
            tops[i].append(m)
    exact = None
    for rank in ranks:
        picked = jnp.sum((rank < float(k)).astype(F32), axis=0, keepdims=True)
        ok = jnp.all(picked == float(k))
        exact = ok if exact is None else jnp.logical_and(exact, ok)
    return tuple(ranks), tuple(jnp.concatenate(t, axis=0) for t in tops), exact


def _merge_top_k(t1, t2, k):
    aidx = lax.broadcasted_iota(jnp.int32, t1.shape, 0).astype(F32)
    nb = jnp.zeros(t1.shape, F32)
    head = t1 + t2[0:1]
    top = head[0:1]
    z = jnp.zeros_like(top)
    for _ in range(k):
        m = jnp.max(head, axis=0, keepdims=True)
        first = jnp.min(jnp.where(head == m, aidx, float(k)), axis=0, keepdims=True)
        sel = aidx == first
        z = z + jnp.exp(m - top)
        nb = jnp.where(sel, nb + 1.0, nb)
        nxt = jnp.sum(jnp.where(sel, nb, 0.0), axis=0, keepdims=True)
        t2n = jnp.sum(jnp.where(aidx == nxt, t2, 0.0), axis=0, keepdims=True)
        t2n = jnp.where(nxt >= float(k), -jnp.inf, t2n)
        head = jnp.where(sel, t1 + t2n, head)
    return nb, z


def _route_kernel(x_ref, wq0_ref, wqn_ref, k1_ref, k2_ref, u_ref, v_ref,
                  r2_ref, b2_ref, nb1_ref, a1_ref, ub_ref, vt_ref, q_ref, exact_ref):
    ub_ref[...] = u_ref[...].astype(BF16)
    vt_ref[0] = v_ref[...].T.astype(BF16)
    tc = x_ref.shape[0]
    half = k1_ref.shape[-1]
    k = PEER_TOPK
    h = pl.program_id(1)
    slot = h % 2

    @pl.when(h == 0)
    def _():
        q_ref[0] = _dot(x_ref[...], wq0_ref[...]).astype(BF16)

    k1 = k1_ref[0, 0].astype(BF16)
    k2 = k2_ref[0, 0].astype(BF16)
    nt = (((1,), (1,)), ((), ()))

    def chunk(c, tie_break):
        start = c * ROUTE_CHUNK
        ts = pl.ds(start if isinstance(c, int) else pl.multiple_of(start, ROUTE_CHUNK), ROUTE_CHUNK)
        qc = q_ref[slot, ts, :]
        s1 = lax.dot_general(k1, qc[:, :half], nt, preferred_element_type=F32)
        s2 = lax.dot_general(k2, qc[:, half:], nt, preferred_element_type=F32)
        (rank1, rank2), (t1, t2), exact = _top_k_rows((s1, s2), k, tie_break)
        nb, z = _merge_top_k(t1, t2, k)
        rank1 = rank1.astype(BF16)
        nb = nb.astype(BF16)
        nb1 = jnp.zeros(rank1.shape, BF16)
        for a in range(k):
            nb1 = jnp.where(rank1 == a, nb[a:a + 1], nb1)
        r2_ref[0, :, ts] = rank2.astype(BF16)
        b2_ref[0, :, ts] = jnp.exp(s2 - t2[0:1]).astype(BF16)
        nb1_ref[0, :, ts] = nb1.astype(F32)
        a1_ref[0, :, ts] = jnp.exp(s1 - t1[0:1]) * (0.5 / z)
        return exact

    for c in range(tc // ROUTE_CHUNK):
        exact_ref[c] = chunk(c, False).astype(jnp.int32)
    q_ref[1 - slot] = _dot(x_ref[...], wqn_ref[...]).astype(BF16)

    def redo(c, carry):
        @pl.when(exact_ref[c] == 0)
        def _():
            chunk(c, True)
        return carry

    lax.fori_loop(0, tc // ROUTE_CHUNK, redo, 0)


def _route(x1b, wq, keys, u, v, *, tc, eb):
    t, d = x1b.shape
    heads, _, nkeys, half = keys.shape
    nexp = u.shape[0]
    steps = (t // tc) * heads
    assert nexp % steps == 0 and (nexp // steps) % LANES == 0
    er = nexp // steps
    assert eb % er == 0
    per_slab = eb // er
    out = pl.BlockSpec((1, nkeys, tc), lambda i, h: (h, 0, i))
    shape = jax.ShapeDtypeStruct((heads, nkeys, t), F32)
    shape_bf16 = jax.ShapeDtypeStruct((heads, nkeys, t), BF16)
    rows = pl.BlockSpec((er, d), lambda i, h: (i * heads + h, 0))
    return pl.pallas_call(
        _route_kernel,
        grid=(t // tc, heads),
        in_specs=[pl.BlockSpec((tc, d), lambda i, h: (i, 0)),
                  pl.BlockSpec((d, 2 * half), lambda i, h: (0, 0)),
                  pl.BlockSpec((d, 2 * half), lambda i, h: (0, jnp.minimum(h + 1, heads - 1))),
                  pl.BlockSpec((1, 1, nkeys, half), lambda i, h: (h, 0, 0, 0)),
                  pl.BlockSpec((1, 1, nkeys, half), lambda i, h: (h, 1, 0, 0)),
                  rows, rows],
        out_specs=[out, out, out, out, rows,
                   pl.BlockSpec((1, d, er), lambda i, h: ((i * heads + h) // per_slab, 0, (i * heads + h) % per_slab))],
        out_shape=[shape_bf16, shape_bf16, shape, shape,
                   jax.ShapeDtypeStruct((nexp, d), BF16), jax.ShapeDtypeStruct((nexp // eb, d, eb), BF16)],
        scratch_shapes=[pltpu.VMEM((2, tc, 2 * half), BF16), pltpu.SMEM((tc // ROUTE_CHUNK,), jnp.int32)],
        compiler_params=_params("parallel", "arbitrary"),
        name="peer_route",
    )(x1b, wq, wq, keys, keys, u, v)


def _experts_kernel(xt_ref, u_ref, vt_ref, r2_ref, b2_ref, nb1_ref, a1_ref, y_ref, acc_ref, *, rows_per_step):
    s = pl.program_id(1)
    heads = r2_ref.shape[0]

    @pl.when(s == 0)
    def _():
        acc_ref[...] = jnp.zeros_like(acc_ref)

    nkeys = r2_ref.shape[1]
    sub = EXPERT_SUB_ROWS
    acts = []
    for q in range(rows_per_step // sub):
        wts = []
        for r in range(q * sub, (q + 1) * sub):
            i1 = s * rows_per_step + r
            wt = None
            for hd in range(heads):
                nb_row = nb1_ref[hd, pl.ds(i1, 1), :].astype(BF16)
                a_row = a1_ref[hd, pl.ds(i1, 1), :].astype(BF16)
                zero = jnp.zeros((), BF16)
                term = jnp.where(r2_ref[hd] < nb_row, b2_ref[hd], zero) * a_row
                wt = term if wt is None else wt + term
            wts.append(wt)
        w = jnp.concatenate(wts, axis=0)
        ht = _dot(u_ref[q * sub * nkeys:(q + 1) * sub * nkeys, :], xt_ref[...])
        two_gelu = ht * (1.0 + lax.erf(ht * SQRT_HALF))
        acts.append(w * two_gelu.astype(BF16))
    act = jnp.concatenate(acts, axis=0)
    acc_ref[...] += _dot(vt_ref[0], act)

    @pl.when(s == pl.num_programs(1) - 1)
    def _():
        y_ref[...] = acc_ref[...].T


def _experts(x1t, ub, vt, r2, b2, nb1, a1, *, tm, rows_per_step):
    d, t = x1t.shape
    heads, nkeys, _ = r2.shape
    eb = rows_per_step * nkeys
    route = pl.BlockSpec((heads, nkeys, tm), lambda i, s: (0, 0, i))
    return pl.pallas_call(
        functools.partial(_experts_kernel, rows_per_step=rows_per_step),
        grid=(t // tm, nkeys // rows_per_step),
        in_specs=[pl.BlockSpec((d, tm), lambda i, s: (0, i)),
                  pl.BlockSpec((eb, d), lambda i, s: (s, 0)),
                  pl.BlockSpec((1, d, eb), lambda i, s: (s, 0, 0)),
                  route, route, route, route],
        out_specs=pl.BlockSpec((tm, d), lambda i, s: (i, 0)),
        out_shape=jax.ShapeDtypeStruct((t, d), F32),
        scratch_shapes=[pltpu.VMEM((d, tm), F32)],
        compiler_params=_params("parallel", "arbitrary"),
        name="peer_experts",
    )(x1t, ub, vt, r2, b2, nb1, a1)


def _tail_kernel(x1_ref, y_ref, p_ref, g_ref, b_ref, wg_ref, wp_ref, o_ref, *, alpha):
    x2 = _layer_norm(alpha * x1_ref[...] + y_ref[...], g_ref[...], b_ref[...])
    gate = _sigmoid(_dot(x2.astype(BF16), wg_ref[...]))
    proj = _dot(p_ref[...].astype(BF16), wp_ref[...])
    o_ref[...] = x2 + gate * proj


def _tail(x1, y, p, g, b, wg, wp, *, alpha, tm):
    t, d = x1.shape
    pd = p.shape[1]
    row = pl.BlockSpec((tm, d), lambda i: (i, 0))
    vec = pl.BlockSpec((1, d), lambda i: (0, 0))
    return pl.pallas_call(
        functools.partial(_tail_kernel, alpha=alpha),
        grid=(t // tm,),
        in_specs=[row, row, pl.BlockSpec((tm, pd), lambda i: (i, 0)), vec, vec,
                  pl.BlockSpec((d, d), lambda i: (0, 0)), pl.BlockSpec((pd, d), lambda i: (0, 0))],
        out_specs=row,
        out_shape=jax.ShapeDtypeStruct((t, d), F32),
        compiler_params=_params("parallel"),
        name="ln2_ple",
    )(x1, y, p, g, b, wg, wp)


def kernel(x, p, w_in, b_in, gm_ln_g, gm_ln_b, gm_ws, gm_bs, w_gm_out, cv_w, cv_b, cv_ln_g, cv_ln_b,
           w_cv_out, w_o, ln1_g, ln1_b, peer_wq, peer_keys, peer_u, peer_v, ln2_g, ln2_b,
           ple_w_gate, ple_w_proj):
    bsz, seq, d = x.shape
    depth = w_in.shape[0]
    t = bsz * seq
    alpha = (2.0 * depth) ** 0.25
    groups, blk, _ = gm_ws.shape[1:]
    gdim = d // groups
    tm = min(512, seq)

    xf = x.reshape(t, d)
    for i in range(depth):
        row = lambda a: a[i].reshape(1, -1)
        bs_b = jnp.broadcast_to(gm_bs[i][:, :, None], (groups, blk, gdim))
        (sg, glu, ga, gb), (w_a, w_b, w_ob, wq_b, wg_b) = _inproj(
            xf, w_in[i].astype(BF16), row(b_in), row(gm_ln_g), row(gm_ln_b), gm_ws[i], bs_b,
            [w_gm_out[i], w_cv_out[i], w_o[i], peer_wq[i], ple_w_gate[i]], tm=min(1024, t), tn=gdim)
        m = _mix(sg, glu, cv_w[i], row(cv_b), row(cv_ln_g), row(cv_ln_b), w_a, w_b, ga, gb,
                 seq=seq, tm=tm, tn=1024)
        x1, x1b, x1t = _oproj(m, xf, w_ob, row(ln1_g), row(ln1_b), alpha=alpha, tm=tm)
        rows_per_step = 8
        r2, b2, nb1, a1, ub, vt = _route(x1b, wq_b, peer_keys[i], peer_u[i], peer_v[i], tc=min(1024, t),
                                         eb=rows_per_step * peer_keys.shape[3])
        y = _experts(x1t, ub, vt, r2, b2, nb1, a1, tm=tm, rows_per_step=rows_per_step)
        xf = _tail(x1, y, p[i].reshape(t, -1), row(ln2_g), row(ln2_b),
                   wg_b, ple_w_proj[i].astype(BF16), alpha=alpha, tm=tm)
    return xf.reshape(bsz, seq, d)
```

```python
import functools

import jax
import jax.numpy as jnp
from jax import lax
from jax.experimental import pallas as pl
from jax.experimental.pallas import tpu as pltpu

LN_EPS = 1e-5
CHUNK = 64
PEER_TOPK = 16
SQRT_HALF = 0.7071067811865476
LANES = 128
SUBLANES = 8
BF16_ROWS = 16
EXPERT_SUB_ROWS = 2
ROUTE_CHUNK = 256
HALO = 32
VMEM_LIMIT = 56 * 1024 * 1024

F32 = jnp.float32
BF16 = jnp.bfloat16


def _gelu(x):
    return 0.5 * x * (1.0 + lax.erf(x * SQRT_HALF))


def _sigmoid(x):
    return 1.0 / (1.0 + jnp.exp(-x))


def _layer_norm(x, g, b):
    mu = jnp.mean(x, axis=-1, keepdims=True)
    d = x - mu
    var = jnp.mean(d * d, axis=-1, keepdims=True)
    return d * lax.rsqrt(var + LN_EPS) * g + b


def _dot(a, b):
    return jnp.dot(a, b, preferred_element_type=F32)


def _params(*sem):
    return pltpu.CompilerParams(dimension_semantics=sem, vmem_limit_bytes=VMEM_LIMIT)


N_INPROJ_INPUTS = 10


def _inproj_kernel(*refs, n_cast):
    (x_ref, wu_ref, wv_ref, wa_ref, wb_ref, wga_ref, wgb_ref, vec_ref, ws_ref, bs_ref) = refs[:N_INPROJ_INPUTS]
    bu_ref, bv_ref, ba_ref, bb_ref, bga_ref, bgb_ref, lng_ref, lnb_ref = (
        vec_ref.at[r:r + 1, :] for r in range(8))
    cast_in = refs[N_INPROJ_INPUTS:N_INPROJ_INPUTS + n_cast]
    sg_ref, glu_ref, ga_ref, gb_ref = refs[N_INPROJ_INPUTS + n_cast:N_INPROJ_INPUTS + n_cast + 4]
    cast_out = refs[N_INPROJ_INPUTS + n_cast + 4:N_INPROJ_INPUTS + 2 * n_cast + 4]
    xb_ref = refs[-1]

    for src, dst in zip(cast_in, cast_out):
        dst[...] = src[...].astype(BF16)

    @pl.when(pl.program_id(1) == 0)
    def _():
        xb_ref[...] = x_ref[...].astype(BF16)

    x = xb_ref[...]
    tm = x.shape[0]
    blk = ws_ref.shape[1]

    def proj(w_ref, b_ref):
        return _dot(x, w_ref[...]) + b_ref[...]

    u = _gelu(proj(wu_ref, bu_ref))
    v = _gelu(proj(wv_ref, bv_ref))
    vn = _layer_norm(v, lng_ref[...], lnb_ref[...]).astype(BF16)
    row = lax.broadcasted_iota(jnp.int32, (blk, blk), 0)
    col = lax.broadcasted_iota(jnp.int32, (blk, blk), 1)
    w = jnp.where(col // CHUNK <= row // CHUNK, ws_ref[0], 0.0).astype(BF16)
    bs = bs_ref[0]
    for nb in range(tm // blk):
        rows = slice(nb * blk, (nb + 1) * blk)
        mixed = _dot(w, vn[rows]) + bs
        sg_ref[rows, :] = (u[rows] * mixed).astype(BF16)

    glu_ref[...] = (proj(wa_ref, ba_ref) * _sigmoid(proj(wb_ref, bb_ref))).astype(BF16)
    ga_ref[...] = _sigmoid(proj(wga_ref, bga_ref)).astype(BF16)
    gb_ref[...] = _sigmoid(proj(wgb_ref, bgb_ref)).astype(BF16)


def _inproj(x, w_in, b_in, ln_g, ln_b, ws, bs_b, later_weights, *, tm, tn):
    t, d = x.shape
    nseg = w_in.shape[1] // d
    ncol = d // tn
    steps = (t // tm) * ncol
    cast_specs, cast_shapes = [], []
    for w in later_weights:
        assert w.shape[0] % steps == 0 and (w.shape[0] // steps) % (2 * SUBLANES) == 0
        cast_specs.append(pl.BlockSpec((w.shape[0] // steps, w.shape[1]), lambda i, j: (i * ncol + j, 0)))
        cast_shapes.append(jax.ShapeDtypeStruct(w.shape, BF16))
    w_specs = [pl.BlockSpec((d, tn), functools.partial(lambda i, j, s: (0, s * ncol + j), s=s))
               for s in range(nseg)]
    vecs = jnp.concatenate([b_in.reshape(nseg, d), ln_g, ln_b], axis=0)
    tile = pl.BlockSpec((tm, tn), lambda i, j: (i, j))
    in_specs = [pl.BlockSpec((tm, d), lambda i, j: (i, 0))] + w_specs + [
        pl.BlockSpec((vecs.shape[0], tn), lambda i, j: (0, j)),
        pl.BlockSpec((1,) + ws.shape[1:], lambda i, j: (j, 0, 0)),
        pl.BlockSpec((1,) + bs_b.shape[1:], lambda i, j: (j, 0, 0)),
    ]
    assert len(in_specs) == N_INPROJ_INPUTS
    outs = pl.pallas_call(
        functools.partial(_inproj_kernel, n_cast=len(later_weights)),
        grid=(t // tm, ncol),
        in_specs=in_specs + cast_specs,
        out_specs=[tile, tile, tile, tile] + cast_specs,
        out_shape=[jax.ShapeDtypeStruct((t, d), BF16), jax.ShapeDtypeStruct((t, d), BF16),
                   jax.ShapeDtypeStruct((t, d), BF16), jax.ShapeDtypeStruct((t, d), BF16)] + cast_shapes,
        scratch_shapes=[pltpu.VMEM((tm, d), BF16)],
        compiler_params=_params("parallel", "arbitrary"),
        name="inproj",
    )(x, *([w_in] * nseg), vecs, ws, bs_b, *later_weights)
    return outs[:4], outs[4:]


def _mix_kernel(sg_ref, glu_ref, halo_ref, cw_ref, cb_ref, clg_ref, clb_ref,
                wa_ref, wb_ref, ga_ref, gb_ref, m_ref, buf_ref, c_ref, cn_ref, *, blocks_per_seq):
    i = pl.program_id(0)
    j = pl.program_id(1)
    tm, d = glu_ref.shape
    taps = cw_ref.shape[0]

    @pl.when(j == 0)
    def _():
        keep = (i % blocks_per_seq != 0).astype(F32)
        buf_ref[0:HALO, :] = halo_ref[...].astype(F32) * keep
        buf_ref[HALO:HALO + tm, :] = glu_ref[...].astype(F32)
        buf_ref[HALO + tm:, :] = jnp.zeros((SUBLANES, d), F32)
        base = HALO - (taps - 1)

        def cols(c, carry):
            cs = pl.ds(pl.multiple_of(c * LANES, LANES), LANES)
            acc = jnp.broadcast_to(cb_ref[:, cs], (tm, LANES))
            for sh in range(SUBLANES):
                part = None
                for k in range(taps):
                    off = base + k
                    if off % SUBLANES == sh:
                        term = cw_ref[k:k + 1, cs] * buf_ref[off - sh:off - sh + tm + SUBLANES, cs]
                        part = term if part is None else part + term
                if part is not None:
                    acc = acc + part[sh:sh + tm]
            c_ref[:, cs] = acc
            return carry

        lax.fori_loop(0, d // LANES, cols, 0)
        c = _layer_norm(c_ref[...], clg_ref[...], clb_ref[...])
        cn_ref[...] = (c * _sigmoid(c)).astype(BF16)

    tn = m_ref.shape[1]
    cols = pl.ds(pl.multiple_of(j * tn, tn), tn)
    ya = _dot(sg_ref[...], wa_ref[:, cols])
    yb = _dot(cn_ref[...], wb_ref[:, cols])
    m_ref[...] = (ga_ref[...].astype(F32) * ya + gb_ref[...].astype(F32) * yb).astype(BF16)


def _mix(sg, glu, cv_w, cv_b, cv_ln_g, cv_ln_b, wa, wb, ga, gb, *, seq, tm, tn):
    t, d = glu.shape
    assert seq % tm == 0 and tm % HALO == 0 and cv_w.shape[0] - 1 <= HALO
    hb = tm // HALO
    row = pl.BlockSpec((tm, d), lambda i, j: (i, 0))
    vec = pl.BlockSpec((1, d), lambda i, j: (0, 0))
    wcol = pl.BlockSpec((d, d), lambda i, j: (0, 0), pipeline_mode=pl.Buffered(1))
    tile = pl.BlockSpec((tm, tn), lambda i, j: (i, j))
    return pl.pallas_call(
        functools.partial(_mix_kernel, blocks_per_seq=seq // tm),
        grid=(t // tm, d // tn),
        in_specs=[row, row,
                  pl.BlockSpec((HALO, d), lambda i, j: (jnp.maximum(i * hb - 1, 0), 0)),
                  pl.BlockSpec(cv_w.shape, lambda i, j: (0, 0)),
                  vec, vec, vec, wcol, wcol, tile, tile],
        out_specs=tile,
        out_shape=jax.ShapeDtypeStruct((t, d), BF16),
        scratch_shapes=[pltpu.VMEM((HALO + tm + SUBLANES, d), F32), pltpu.VMEM((tm, d), F32),
                        pltpu.VMEM((tm, d), BF16)],
        compiler_params=_params("parallel", "arbitrary"),
        name="branch_mix",
    )(sg, glu, glu, cv_w, cv_b, cv_ln_g, cv_ln_b, wa, wb, ga, gb)


def _oproj_kernel(m_ref, x_ref, wo_ref, g_ref, b_ref, x1_ref, x1b_ref, x1t_ref, *, alpha):
    mix = _dot(m_ref[...], wo_ref[...])
    x1 = _layer_norm(alpha * x_ref[...] + mix, g_ref[...], b_ref[...])
    x1_ref[...] = x1
    x1b_ref[...] = x1.astype(BF16)
    x1t_ref[...] = x1.T.astype(BF16)


def _oproj(m, x, wo, g, b, *, alpha, tm):
    t, d = x.shape
    row = pl.BlockSpec((tm, d), lambda i: (i, 0))
    vec = pl.BlockSpec((1, d), lambda i: (0, 0))
    return pl.pallas_call(
        functools.partial(_oproj_kernel, alpha=alpha),
        grid=(t // tm,),
        in_specs=[row, row, pl.BlockSpec((d, d), lambda i: (0, 0)), vec, vec],
        out_specs=[row, row, pl.BlockSpec((d, tm), lambda i: (0, i))],
        out_shape=[jax.ShapeDtypeStruct((t, d), F32), jax.ShapeDtypeStruct((t, d), BF16),
                   jax.ShapeDtypeStruct((d, t), BF16)],
        compiler_params=_params("parallel"),
        name="out_proj_ln1",
    )(m, x, wo, g, b)


def _top_k_rows(ss, k, tie_break):
    n = ss[0].shape[0]
    idx = lax.broadcasted_iota(jnp.int32, ss[0].shape, 0).astype(F32)
    ranks = [jnp.full(s.shape, float(k), F32) for s in ss]
    works = list(ss)
    tops = [[] for _ in ss]
    for r in range(k):
        for i in range(len(ss)):
            m = jnp.max(works[i], axis=0, keepdims=True)
            sel = works[i] == m
            if tie_break:
                first = jnp.min(jnp.where(sel, idx, float(n)), axis=0, keepdims=True)
                sel = idx == first
            ranks[i] = jnp.where(sel, float(r), ranks[i])
            works[i] = jnp.where(sel, -jnp.inf, works[i])
            tops[i].append(m)
    exact = None
    for rank in ranks:
        picked = jnp.sum((rank < float(k)).astype(F32), axis=0, keepdims=True)
        ok = jnp.all(picked == float(k))
        exact = ok if exact is None else jnp.logical_and(exact, ok)
    return tuple(ranks), tuple(jnp.concatenate(t, axis=0) for t in tops), exact


def _merge_top_k(t1, t2, k):
    aidx = lax.broadcasted_iota(jnp.int32, t1.shape, 0).astype(F32)
    nb = jnp.zeros(t1.shape, F32)
    head = t1 + t2[0:1]
    top = head[0:1]
    z = jnp.zeros_like(top)
    for _ in range(k):
        m = jnp.max(head, axis=0, keepdims=True)
        first = jnp.min(jnp.where(head == m, aidx, float(k)), axis=0, keepdims=True)
        sel = aidx == first
        z = z + jnp.exp(m - top)
        nb = jnp.where(sel, nb + 1.0, nb)
        nxt = jnp.sum(jnp.where(sel, nb, 0.0), axis=0, keepdims=True)
        t2n = jnp.sum(jnp.where(aidx == nxt, t2, 0.0), axis=0, keepdims=True)
        t2n = jnp.where(nxt >= float(k), -jnp.inf, t2n)
        head = jnp.where(sel, t1 + t2n, head)
    return nb, z


def _route_kernel(x_ref, wq0_ref, wqn_ref, k1_ref, k2_ref, u_ref, v_ref,
                  r2_ref, b2_ref, nb1_ref, a1_ref, ub_ref, vt_ref, q_ref, exact_ref):
    ub_ref[...] = u_ref[...].astype(BF16)
    vt_ref[...] = v_ref[...].T.astype(BF16)
    tc = x_ref.shape[0]
    half = k1_ref.shape[-1]
    k = PEER_TOPK
    h = pl.program_id(1)
    slot = h % 2

    @pl.when(h == 0)
    def _():
        q_ref[0] = _dot(x_ref[...], wq0_ref[...]).astype(BF16)

    k1 = k1_ref[0, 0].astype(BF16)
    k2 = k2_ref[0, 0].astype(BF16)
    nt = (((1,), (1,)), ((), ()))

    def chunk(c, tie_break):
        start = c * ROUTE_CHUNK
        ts = pl.ds(start if isinstance(c, int) else pl.multiple_of(start, ROUTE_CHUNK), ROUTE_CHUNK)
        qc = q_ref[slot, ts, :]
        s1 = lax.dot_general(k1, qc[:, :half], nt, preferred_element_type=F32)
        s2 = lax.dot_general(k2, qc[:, half:], nt, preferred_element_type=F32)
        (rank1, rank2), (t1, t2), exact = _top_k_rows((s1, s2), k, tie_break)
        nb, z = _merge_top_k(t1, t2, k)
        rank1 = rank1.astype(BF16)
        nb = nb.astype(BF16)
        nb1 = jnp.zeros(rank1.shape, BF16)
        for a in range(k):
            nb1 = jnp.where(rank1 == a, nb[a:a + 1], nb1)
        r2_ref[0, :, ts] = rank2.astype(BF16)
        b2_ref[0, :, ts] = jnp.exp(s2 - t2[0:1]).astype(BF16)
        nb1_ref[0, :, ts] = nb1.astype(F32)
        a1_ref[0, :, ts] = jnp.exp(s1 - t1[0:1]) * (0.5 / z)
        return exact

    for c in range(tc // ROUTE_CHUNK):
        exact_ref[c] = chunk(c, False).astype(jnp.int32)
    q_ref[1 - slot] = _dot(x_ref[...], wqn_ref[...]).astype(BF16)

    def redo(c, carry):
        @pl.when(exact_ref[c] == 0)
        def _():
            chunk(c, True)
        return carry

    lax.fori_loop(0, tc // ROUTE_CHUNK, redo, 0)


def _route(x1b, wq, keys, u, v, *, tc):
    t, d = x1b.shape
    heads, _, nkeys, half = keys.shape
    nexp = u.shape[0]
    steps = (t // tc) * heads
    assert nexp % steps == 0 and (nexp // steps) % LANES == 0
    er = nexp // steps
    out = pl.BlockSpec((1, nkeys, tc), lambda i, h: (h, 0, i))
    shape = jax.ShapeDtypeStruct((heads, nkeys, t), F32)
    shape_bf16 = jax.ShapeDtypeStruct((heads, nkeys, t), BF16)
    rows = pl.BlockSpec((er, d), lambda i, h: (i * heads + h, 0))
    return pl.pallas_call(
        _route_kernel,
        grid=(t // tc, heads),
        in_specs=[pl.BlockSpec((tc, d), lambda i, h: (i, 0)),
                  pl.BlockSpec((d, 2 * half), lambda i, h: (0, 0)),
                  pl.BlockSpec((d, 2 * half), lambda i, h: (0, jnp.minimum(h + 1, heads - 1))),
                  pl.BlockSpec((1, 1, nkeys, half), lambda i, h: (h, 0, 0, 0)),
                  pl.BlockSpec((1, 1, nkeys, half), lambda i, h: (h, 1, 0, 0)),
                  rows, rows],
        out_specs=[out, out, out, out, rows, pl.BlockSpec((d, er), lambda i, h: (0, i * heads + h))],
        out_shape=[shape_bf16, shape_bf16, shape, shape,
                   jax.ShapeDtypeStruct((nexp, d), BF16), jax.ShapeDtypeStruct((d, nexp), BF16)],
        scratch_shapes=[pltpu.VMEM((2, tc, 2 * half), BF16), pltpu.SMEM((tc // ROUTE_CHUNK,), jnp.int32)],
        compiler_params=_params("parallel", "arbitrary"),
        name="peer_route",
    )(x1b, wq, wq, keys, keys, u, v)


def _experts_kernel(xt_ref, u_ref, vt_ref, r2_ref, b2_ref, nb1_ref, a1_ref, y_ref, acc_ref, *, rows_per_step):
    s = pl.program_id(1)
    heads = r2_ref.shape[0]

    @pl.when(s == 0)
    def _():
        acc_ref[...] = jnp.zeros_like(acc_ref)

    nkeys = r2_ref.shape[1]
    sub = EXPERT_SUB_ROWS
    acts = []
    for q in range(rows_per_step // sub):
        wts = []
        for r in range(q * sub, (q + 1) * sub):
            i1 = s * rows_per_step + r
            wt = None
            tm = xt_ref.shape[1]
            for hd in range(heads):
                nb_row = jnp.broadcast_to(nb1_ref[hd, pl.ds(i1, 1), :], (BF16_ROWS, tm)).astype(BF16)
                a_row = jnp.broadcast_to(a1_ref[hd, pl.ds(i1, 1), :], (BF16_ROWS, tm)).astype(BF16)
                zero = jnp.zeros((), BF16)
                r2 = r2_ref[hd].reshape(nkeys // BF16_ROWS, BF16_ROWS, tm)
                b2 = b2_ref[hd].reshape(nkeys // BF16_ROWS, BF16_ROWS, tm)
                term = jnp.where(r2 < nb_row[None], b2, zero) * a_row[None]
                wt = term if wt is None else wt + term
            wts.append(wt.reshape(nkeys, tm))
        w = jnp.concatenate(wts, axis=0)
        ht = _dot(u_ref[q * sub * nkeys:(q + 1) * sub * nkeys, :], xt_ref[...])
        two_gelu = ht * (1.0 + lax.erf(ht * SQRT_HALF))
        acts.append(w * two_gelu.astype(BF16))
    act = jnp.concatenate(acts, axis=0)
    acc_ref[...] += _dot(vt_ref[...], act)

    @pl.when(s == pl.num_programs(1) - 1)
    def _():
        y_ref[...] = acc_ref[...].T


def _experts(x1t, ub, vt, r2, b2, nb1, a1, *, tm, rows_per_step):
    d, t = x1t.shape
    heads, nkeys, _ = r2.shape
    eb = rows_per_step * nkeys
    route = pl.BlockSpec((heads, nkeys, tm), lambda i, s: (0, 0, i))
    return pl.pallas_call(
        functools.partial(_experts_kernel, rows_per_step=rows_per_step),
        grid=(t // tm, nkeys // rows_per_step),
        in_specs=[pl.BlockSpec((d, tm), lambda i, s: (0, i)),
                  pl.BlockSpec((eb, d), lambda i, s: (s, 0)),
                  pl.BlockSpec((d, eb), lambda i, s: (0, s)),
                  route, route, route, route],
        out_specs=pl.BlockSpec((tm, d), lambda i, s: (i, 0)),
        out_shape=jax.ShapeDtypeStruct((t, d), F32),
        scratch_shapes=[pltpu.VMEM((d, tm), F32)],
        compiler_params=_params("parallel", "arbitrary"),
        name="peer_experts",
    )(x1t, ub, vt, r2, b2, nb1, a1)


def _tail_kernel(x1_ref, y_ref, p_ref, g_ref, b_ref, wg_ref, wp_ref, o_ref, *, alpha):
    x2 = _layer_norm(alpha * x1_ref[...] + y_ref[...], g_ref[...], b_ref[...])
    gate = _sigmoid(_dot(x2.astype(BF16), wg_ref[...]))
    proj = _dot(p_ref[...].astype(BF16), wp_ref[...])
    o_ref[...] = x2 + gate * proj


def _tail(x1, y, p, g, b, wg, wp, *, alpha, tm):
    t, d = x1.shape
    pd = p.shape[1]
    row = pl.BlockSpec((tm, d), lambda i: (i, 0))
    vec = pl.BlockSpec((1, d), lambda i: (0, 0))
    return pl.pallas_call(
        functools.partial(_tail_kernel, alpha=alpha),
        grid=(t // tm,),
        in_specs=[row, row, pl.BlockSpec((tm, pd), lambda i: (i, 0)), vec, vec,
                  pl.BlockSpec((d, d), lambda i: (0, 0)), pl.BlockSpec((pd, d), lambda i: (0, 0))],
        out_specs=row,
        out_shape=jax.ShapeDtypeStruct((t, d), F32),
        compiler_params=_params("parallel"),
        name="ln2_ple",
    )(x1, y, p, g, b, wg, wp)


def kernel(x, p, w_in, b_in, gm_ln_g, gm_ln_b, gm_ws, gm_bs, w_gm_out, cv_w, cv_b, cv_ln_g, cv_ln_b,
           w_cv_out, w_o, ln1_g, ln1_b, peer_wq, peer_keys, peer_u, peer_v, ln2_g, ln2_b,
           ple_w_gate, ple_w_proj):
    bsz, seq, d = x.shape
    depth = w_in.shape[0]
    t = bsz * seq
    alpha = (2.0 * depth) ** 0.25
    groups, blk, _ = gm_ws.shape[1:]
    gdim = d // groups
    tm = min(512, seq)

    xf = x.reshape(t, d)
    for i in range(depth):
        row = lambda a: a[i].reshape(1, -1)
        bs_b = jnp.broadcast_to(gm_bs[i][:, :, None], (groups, blk, gdim))
        (sg, glu, ga, gb), (w_a, w_b, w_ob, wq_b, wg_b) = _inproj(
            xf, w_in[i].astype(BF16), row(b_in), row(gm_ln_g), row(gm_ln_b), gm_ws[i], bs_b,
            [w_gm_out[i], w_cv_out[i], w_o[i], peer_wq[i], ple_w_gate[i]], tm=min(1024, t), tn=gdim)
        m = _mix(sg, glu, cv_w[i], row(cv_b), row(cv_ln_g), row(cv_ln_b), w_a, w_b, ga, gb,
                 seq=seq, tm=tm, tn=1024)
        x1, x1b, x1t = _oproj(m, xf, w_ob, row(ln1_g), row(ln1_b), alpha=alpha, tm=tm)
        r2, b2, nb1, a1, ub, vt = _route(x1b, wq_b, peer_keys[i], peer_u[i], peer_v[i], tc=min(1024, t))
        y = _experts(x1t, ub, vt, r2, b2, nb1, a1, tm=tm, rows_per_step=8)
        xf = _tail(x1, y, p[i].reshape(t, -1), row(ln2_g), row(ln2_b),
                   wg_b, ple_w_proj[i].astype(BF16), alpha=alpha, tm=tm)
    return xf.reshape(bsz, seq, d)
```

```python
import functools

import jax
import jax.numpy as jnp
from jax import lax
from jax.experimental import pallas as pl
from jax.experimental.pallas import tpu as pltpu

LN_EPS = 1e-5
CHUNK = 64
PEER_TOPK = 16
SQRT_HALF = 0.7071067811865476
LANES = 128
SUBLANES = 8
BF16_ROWS = 16
EXPERT_SUB_ROWS = 2
ROUTE_CHUNK = 256
HALO = 32
VMEM_LIMIT = 56 * 1024 * 1024

F32 = jnp.float32
BF16 = jnp.bfloat16


def _gelu(x):
    return 0.5 * x * (1.0 + lax.erf(x * SQRT_HALF))


def _sigmoid(x):
    return 1.0 / (1.0 + jnp.exp(-x))


def _layer_norm(x, g, b):
    mu = jnp.mean(x, axis=-1, keepdims=True)
    d = x - mu
    var = jnp.mean(d * d, axis=-1, keepdims=True)
    return d * lax.rsqrt(var + LN_EPS) * g + b


def _dot(a, b):
    return jnp.dot(a, b, preferred_element_type=F32)


def _params(*sem):
    return pltpu.CompilerParams(dimension_semantics=sem, vmem_limit_bytes=VMEM_LIMIT)


N_INPROJ_INPUTS = 10


def _inproj_kernel(*refs, n_cast):
    (x_ref, wu_ref, wv_ref, wa_ref, wb_ref, wga_ref, wgb_ref, vec_ref, ws_ref, bs_ref) = refs[:N_INPROJ_INPUTS]
    bu_ref, bv_ref, ba_ref, bb_ref, bga_ref, bgb_ref, lng_ref, lnb_ref = (
        vec_ref.at[r:r + 1, :] for r in range(8))
    cast_in = refs[N_INPROJ_INPUTS:N_INPROJ_INPUTS + n_cast]
    sg_ref, glu_ref, ga_ref, gb_ref = refs[N_INPROJ_INPUTS + n_cast:N_INPROJ_INPUTS + n_cast + 4]
    cast_out = refs[N_INPROJ_INPUTS + n_cast + 4:N_INPROJ_INPUTS + 2 * n_cast + 4]
    xb_ref = refs[-1]

    for src, dst in zip(cast_in, cast_out):
        dst[...] = src[...].astype(BF16)

    @pl.when(pl.program_id(1) == 0)
    def _():
        xb_ref[...] = x_ref[...].astype(BF16)

    x = xb_ref[...]
    tm = x.shape[0]
    blk = ws_ref.shape[1]

    def proj(w_ref, b_ref):
        return _dot(x, w_ref[...]) + b_ref[...]

    u = _gelu(proj(wu_ref, bu_ref))
    v = _gelu(proj(wv_ref, bv_ref))
    vn = _layer_norm(v, lng_ref[...], lnb_ref[...]).astype(BF16)
    row = lax.broadcasted_iota(jnp.int32, (blk, blk), 0)
    col = lax.broadcasted_iota(jnp.int32, (blk, blk), 1)
    w = jnp.where(col // CHUNK <= row // CHUNK, ws_ref[0], 0.0).astype(BF16)
    bs = bs_ref[0]
    for nb in range(tm // blk):
        rows = slice(nb * blk, (nb + 1) * blk)
        mixed = _dot(w, vn[rows]) + bs
        sg_ref[rows, :] = (u[rows] * mixed).astype(BF16)

    glu_ref[...] = (proj(wa_ref, ba_ref) * _sigmoid(proj(wb_ref, bb_ref))).astype(BF16)
    ga_ref[...] = _sigmoid(proj(wga_ref, bga_ref)).astype(BF16)
    gb_ref[...] = _sigmoid(proj(wgb_ref, bgb_ref)).astype(BF16)


def _inproj(x, w_in, b_in, ln_g, ln_b, ws, bs_b, later_weights, *, tm, tn):
    t, d = x.shape
    nseg = w_in.shape[1] // d
    ncol = d // tn
    steps = (t // tm) * ncol
    cast_specs, cast_shapes = [], []
    for w in later_weights:
        assert w.shape[0] % steps == 0 and (w.shape[0] // steps) % (2 * SUBLANES) == 0
        cast_specs.append(pl.BlockSpec((w.shape[0] // steps, w.shape[1]), lambda i, j: (i * ncol + j, 0)))
        cast_shapes.append(jax.ShapeDtypeStruct(w.shape, BF16))
    w_specs = [pl.BlockSpec((d, tn), functools.partial(lambda i, j, s: (0, s * ncol + j), s=s))
               for s in range(nseg)]
    vecs = jnp.concatenate([b_in.reshape(nseg, d), ln_g, ln_b], axis=0)
    tile = pl.BlockSpec((tm, tn), lambda i, j: (i, j))
    in_specs = [pl.BlockSpec((tm, d), lambda i, j: (i, 0))] + w_specs + [
        pl.BlockSpec((vecs.shape[0], tn), lambda i, j: (0, j)),
        pl.BlockSpec((1,) + ws.shape[1:], lambda i, j: (j, 0, 0)),
        pl.BlockSpec((1,) + bs_b.shape[1:], lambda i, j: (j, 0, 0)),
    ]
    assert len(in_specs) == N_INPROJ_INPUTS
    outs = pl.pallas_call(
        functools.partial(_inproj_kernel, n_cast=len(later_weights)),
        grid=(t // tm, ncol),
        in_specs=in_specs + cast_specs,
        out_specs=[tile, tile, tile, tile] + cast_specs,
        out_shape=[jax.ShapeDtypeStruct((t, d), BF16), jax.ShapeDtypeStruct((t, d), BF16),
                   jax.ShapeDtypeStruct((t, d), BF16), jax.ShapeDtypeStruct((t, d), BF16)] + cast_shapes,
        scratch_shapes=[pltpu.VMEM((tm, d), BF16)],
        compiler_params=_params("parallel", "arbitrary"),
        name="inproj",
    )(x, *([w_in] * nseg), vecs, ws, bs_b, *later_weights)
    return outs[:4], outs[4:]


def _mix_kernel(sg_ref, glu_ref, halo_ref, cw_ref, cb_ref, clg_ref, clb_ref,
                wa_ref, wb_ref, ga_ref, gb_ref, m_ref, buf_ref, c_ref, cn_ref, *, blocks_per_seq):
    i = pl.program_id(0)
    j = pl.program_id(1)
    tm, d = glu_ref.shape
    taps = cw_ref.shape[0]

    @pl.when(j == 0)
    def _():
        keep = (i % blocks_per_seq != 0).astype(F32)
        buf_ref[0:HALO, :] = halo_ref[...].astype(F32) * keep
        buf_ref[HALO:HALO + tm, :] = glu_ref[...].astype(F32)
        buf_ref[HALO + tm:, :] = jnp.zeros((SUBLANES, d), F32)
        base = HALO - (taps - 1)

        def cols(c, carry):
            cs = pl.ds(pl.multiple_of(c * LANES, LANES), LANES)
            acc = jnp.broadcast_to(cb_ref[:, cs], (tm, LANES))
            for sh in range(SUBLANES):
                part = None
                for k in range(taps):
                    off = base + k
                    if off % SUBLANES == sh:
                        term = cw_ref[k:k + 1, cs] * buf_ref[off - sh:off - sh + tm + SUBLANES, cs]
                        part = term if part is None else part + term
                if part is not None:
                    acc = acc + part[sh:sh + tm]
            c_ref[:, cs] = acc
            return carry

        lax.fori_loop(0, d // LANES, cols, 0)
        c = _layer_norm(c_ref[...], clg_ref[...], clb_ref[...])
        cn_ref[...] = (c * _sigmoid(c)).astype(BF16)

    tn = m_ref.shape[1]
    cols = pl.ds(pl.multiple_of(j * tn, tn), tn)
    ya = _dot(sg_ref[...], wa_ref[:, cols])
    yb = _dot(cn_ref[...], wb_ref[:, cols])
    m_ref[...] = (ga_ref[...].astype(F32) * ya + gb_ref[...].astype(F32) * yb).astype(BF16)


def _mix(sg, glu, cv_w, cv_b, cv_ln_g, cv_ln_b, wa, wb, ga, gb, *, seq, tm, tn):
    t, d = glu.shape
    assert seq % tm == 0 and tm % HALO == 0 and cv_w.shape[0] - 1 <= HALO
    hb = tm // HALO
    row = pl.BlockSpec((tm, d), lambda i, j: (i, 0))
    vec = pl.BlockSpec((1, d), lambda i, j: (0, 0))
    wcol = pl.BlockSpec((d, d), lambda i, j: (0, 0), pipeline_mode=pl.Buffered(1))
    tile = pl.BlockSpec((tm, tn), lambda i, j: (i, j))
    return pl.pallas_call(
        functools.partial(_mix_kernel, blocks_per_seq=seq // tm),
        grid=(t // tm, d // tn),
        in_specs=[row, row,
                  pl.BlockSpec((HALO, d), lambda i, j: (jnp.maximum(i * hb - 1, 0), 0)),
                  pl.BlockSpec(cv_w.shape, lambda i, j: (0, 0)),
                  vec, vec, vec, wcol, wcol, tile, tile],
        out_specs=tile,
        out_shape=jax.ShapeDtypeStruct((t, d), BF16),
        scratch_shapes=[pltpu.VMEM((HALO + tm + SUBLANES, d), F32), pltpu.VMEM((tm, d), F32),
                        pltpu.VMEM((tm, d), BF16)],
        compiler_params=_params("parallel", "arbitrary"),
        name="branch_mix",
    )(sg, glu, glu, cv_w, cv_b, cv_ln_g, cv_ln_b, wa, wb, ga, gb)


def _oproj_kernel(m_ref, x_ref, wo_ref, g_ref, b_ref, x1_ref, x1b_ref, x1t_ref, *, alpha):
    half = m_ref.shape[0] // 2
    for r in range(2):
        rows = slice(r * half, (r + 1) * half)
        mix = _dot(m_ref[rows, :], wo_ref[...])
        x1 = _layer_norm(alpha * x_ref[rows, :] + mix, g_ref[...], b_ref[...])
        x1_ref[rows, :] = x1
        x1b_ref[rows, :] = x1.astype(BF16)
        x1t_ref[:, rows] = x1.T.astype(BF16)


def _oproj(m, x, wo, g, b, *, alpha, tm):
    t, d = x.shape
    row = pl.BlockSpec((tm, d), lambda i: (i, 0))
    vec = pl.BlockSpec((1, d), lambda i: (0, 0))
    return pl.pallas_call(
        functools.partial(_oproj_kernel, alpha=alpha),
        grid=(t // tm,),
        in_specs=[row, row, pl.BlockSpec((d, d), lambda i: (0, 0)), vec, vec],
        out_specs=[row, row, pl.BlockSpec((d, tm), lambda i: (0, i))],
        out_shape=[jax.ShapeDtypeStruct((t, d), F32), jax.ShapeDtypeStruct((t, d), BF16),
                   jax.ShapeDtypeStruct((d, t), BF16)],
        compiler_params=_params("parallel"),
        name="out_proj_ln1",
    )(m, x, wo, g, b)


def _top_k_rows(ss, k, tie_break):
    n = ss[0].shape[0]
    idx = lax.broadcasted_iota(jnp.int32, ss[0].shape, 0).astype(F32)
    ranks = [jnp.full(s.shape, float(k), F32) for s in ss]
    works = list(ss)
    tops = [[] for _ in ss]
    for r in range(k):
        for i in range(len(ss)):
            m = jnp.max(works[i], axis=0, keepdims=True)
            sel = works[i] == m
            if tie_break:
                first = jnp.min(jnp.where(sel, idx, float(n)), axis=0, keepdims=True)
                sel = idx == first
            ranks[i] = jnp.where(sel, float(r), ranks[i])
            works[i] = jnp.where(sel, -jnp.inf, works[i])
            tops[i].append(m)
    exact = None
    for rank in ranks:
        picked = jnp.sum((rank < float(k)).astype(F32), axis=0, keepdims=True)
        ok = jnp.all(picked == float(k))
        exact = ok if exact is None else jnp.logical_and(exact, ok)
    return tuple(ranks), tuple(jnp.concatenate(t, axis=0) for t in tops), exact


def _merge_top_k(t1, t2, k):
    aidx = lax.broadcasted_iota(jnp.int32, t1.shape, 0).astype(F32)
    nb = jnp.zeros(t1.shape, F32)
    head = t1 + t2[0:1]
    top = head[0:1]
    z = jnp.zeros_like(top)
    for _ in range(k):
        m = jnp.max(head, axis=0, keepdims=True)
        first = jnp.min(jnp.where(head == m, aidx, float(k)), axis=0, keepdims=True)
        sel = aidx == first
        z = z + jnp.exp(m - top)
        nb = jnp.where(sel, nb + 1.0, nb)
        nxt = jnp.sum(jnp.where(sel, nb, 0.0), axis=0, keepdims=True)
        t2n = jnp.sum(jnp.where(aidx == nxt, t2, 0.0), axis=0, keepdims=True)
        t2n = jnp.where(nxt >= float(k), -jnp.inf, t2n)
        head = jnp.where(sel, t1 + t2n, head)
    return nb, z


def _route_kernel(x_ref, wq0_ref, wqn_ref, k1_ref, k2_ref, u_ref, v_ref,
                  r2_ref, b2_ref, nb1_ref, a1_ref, ub_ref, vt_ref, q_ref, exact_ref):
    ub_ref[...] = u_ref[...].astype(BF16)
    vt_ref[...] = v_ref[...].T.astype(BF16)
    tc = x_ref.shape[0]
    half = k1_ref.shape[-1]
    k = PEER_TOPK
    h = pl.program_id(1)
    slot = h % 2

    @pl.when(h == 0)
    def _():
        q_ref[0] = _dot(x_ref[...], wq0_ref[...]).astype(BF16)

    k1 = k1_ref[0, 0].astype(BF16)
    k2 = k2_ref[0, 0].astype(BF16)
    nt = (((1,), (1,)), ((), ()))

    def chunk(c, tie_break):
        start = c * ROUTE_CHUNK
        ts = pl.ds(start if isinstance(c, int) else pl.multiple_of(start, ROUTE_CHUNK), ROUTE_CHUNK)
        qc = q_ref[slot, ts, :]
        s1 = lax.dot_general(k1, qc[:, :half], nt, preferred_element_type=F32)
        s2 = lax.dot_general(k2, qc[:, half:], nt, preferred_element_type=F32)
        (rank1, rank2), (t1, t2), exact = _top_k_rows((s1, s2), k, tie_break)
        nb, z = _merge_top_k(t1, t2, k)
        rank1 = rank1.astype(BF16)
        nb = nb.astype(BF16)
        nb1 = jnp.zeros(rank1.shape, BF16)
        for a in range(k):
            nb1 = jnp.where(rank1 == a, nb[a:a + 1], nb1)
        r2_ref[0, :, ts] = rank2.astype(BF16)
        b2_ref[0, :, ts] = jnp.exp(s2 - t2[0:1]).astype(BF16)
        nb1_ref[0, :, ts] = nb1.astype(F32)
        a1_ref[0, :, ts] = jnp.exp(s1 - t1[0:1]) * (0.5 / z)
        return exact

    for c in range(tc // ROUTE_CHUNK):
        exact_ref[c] = chunk(c, False).astype(jnp.int32)
    q_ref[1 - slot] = _dot(x_ref[...], wqn_ref[...]).astype(BF16)

    def redo(c, carry):
        @pl.when(exact_ref[c] == 0)
        def _():
            chunk(c, True)
        return carry

    lax.fori_loop(0, tc // ROUTE_CHUNK, redo, 0)


def _route(x1b, wq, keys, u, v, *, tc):
    t, d = x1b.shape
    heads, _, nkeys, half = keys.shape
    nexp = u.shape[0]
    steps = (t // tc) * heads
    assert nexp % steps == 0 and (nexp // steps) % LANES == 0
    er = nexp // steps
    out = pl.BlockSpec((1, nkeys, tc), lambda i, h: (h, 0, i))
    shape = jax.ShapeDtypeStruct((heads, nkeys, t), F32)
    shape_bf16 = jax.ShapeDtypeStruct((heads, nkeys, t), BF16)
    rows = pl.BlockSpec((er, d), lambda i, h: (i * heads + h, 0))
    return pl.pallas_call(
        _route_kernel,
        grid=(t // tc, heads),
        in_specs=[pl.BlockSpec((tc, d), lambda i, h: (i, 0)),
                  pl.BlockSpec((d, 2 * half), lambda i, h: (0, 0)),
                  pl.BlockSpec((d, 2 * half), lambda i, h: (0, jnp.minimum(h + 1, heads - 1))),
                  pl.BlockSpec((1, 1, nkeys, half), lambda i, h: (h, 0, 0, 0)),
                  pl.BlockSpec((1, 1, nkeys, half), lambda i, h: (h, 1, 0, 0)),
                  rows, rows],
        out_specs=[out, out, out, out, rows, pl.BlockSpec((d, er), lambda i, h: (0, i * heads + h))],
        out_shape=[shape_bf16, shape_bf16, shape, shape,
                   jax.ShapeDtypeStruct((nexp, d), BF16), jax.ShapeDtypeStruct((d, nexp), BF16)],
        scratch_shapes=[pltpu.VMEM((2, tc, 2 * half), BF16), pltpu.SMEM((tc // ROUTE_CHUNK,), jnp.int32)],
        compiler_params=_params("parallel", "arbitrary"),
        name="peer_route",
    )(x1b, wq, wq, keys, keys, u, v)


def _experts_kernel(xt_ref, u_ref, vt_ref, r2_ref, b2_ref, nb1_ref, a1_ref, y_ref, acc_ref, *, rows_per_step):
    s = pl.program_id(1)
    heads = r2_ref.shape[0]

    @pl.when(s == 0)
    def _():
        acc_ref[...] = jnp.zeros_like(acc_ref)

    nkeys = r2_ref.shape[1]
    sub = EXPERT_SUB_ROWS
    acts = []
    for q in range(rows_per_step // sub):
        wts = []
        for r in range(q * sub, (q + 1) * sub):
            i1 = s * rows_per_step + r
            wt = None
            tm = xt_ref.shape[1]
            for hd in range(heads):
                nb_row = jnp.broadcast_to(nb1_ref[hd, pl.ds(i1, 1), :], (BF16_ROWS, tm)).astype(BF16)
                a_row = jnp.broadcast_to(a1_ref[hd, pl.ds(i1, 1), :], (BF16_ROWS, tm)).astype(BF16)
                zero = jnp.zeros((), BF16)
                r2 = r2_ref[hd].reshape(nkeys // BF16_ROWS, BF16_ROWS, tm)
                b2 = b2_ref[hd].reshape(nkeys // BF16_ROWS, BF16_ROWS, tm)
                term = jnp.where(r2 < nb_row[None], b2, zero) * a_row[None]
                wt = term if wt is None else wt + term
            wts.append(wt.reshape(nkeys, tm))
        w = jnp.concatenate(wts, axis=0)
        ht = _dot(u_ref[q * sub * nkeys:(q + 1) * sub * nkeys, :], xt_ref[...])
        two_gelu = ht * (1.0 + lax.erf(ht * SQRT_HALF))
        acts.append(w * two_gelu.astype(BF16))
    act = jnp.concatenate(acts, axis=0)
    acc_ref[...] += _dot(vt_ref[...], act)

    @pl.when(s == pl.num_programs(1) - 1)
    def _():
        y_ref[...] = acc_ref[...].T


def _experts(x1t, ub, vt, r2, b2, nb1, a1, *, tm, rows_per_step):
    d, t = x1t.shape
    heads, nkeys, _ = r2.shape
    eb = rows_per_step * nkeys
    route = pl.BlockSpec((heads, nkeys, tm), lambda i, s: (0, 0, i))
    return pl.pallas_call(
        functools.partial(_experts_kernel, rows_per_step=rows_per_step),
        grid=(t // tm, nkeys // rows_per_step),
        in_specs=[pl.BlockSpec((d, tm), lambda i, s: (0, i)),
                  pl.BlockSpec((eb, d), lambda i, s: (s, 0)),
                  pl.BlockSpec((d, eb), lambda i, s: (0, s)),
                  route, route, route, route],
        out_specs=pl.BlockSpec((tm, d), lambda i, s: (i, 0)),
        out_shape=jax.ShapeDtypeStruct((t, d), F32),
        scratch_shapes=[pltpu.VMEM((d, tm), F32)],
        compiler_params=_params("parallel", "arbitrary"),
        name="peer_experts",
    )(x1t, ub, vt, r2, b2, nb1, a1)


def _tail_kernel(x1_ref, y_ref, p_ref, g_ref, b_ref, wg_ref, wp_ref, o_ref, *, alpha):
    half = x1_ref.shape[0] // 2
    for r in range(2):
        rows = slice(r * half, (r + 1) * half)
        x2 = _layer_norm(alpha * x1_ref[rows, :] + y_ref[rows, :], g_ref[...], b_ref[...])
        gate = _sigmoid(_dot(x2.astype(BF16), wg_ref[...]))
        proj = _dot(p_ref[rows, :].astype(BF16), wp_ref[...])
        o_ref[rows, :] = x2 + gate * proj


def _tail(x1, y, p, g, b, wg, wp, *, alpha, tm):
    t, d = x1.shape
    pd = p.shape[1]
    row = pl.BlockSpec((tm, d), lambda i: (i, 0))
    vec = pl.BlockSpec((1, d), lambda i: (0, 0))
    return pl.pallas_call(
        functools.partial(_tail_kernel, alpha=alpha),
        grid=(t // tm,),
        in_specs=[row, row, pl.BlockSpec((tm, pd), lambda i: (i, 0)), vec, vec,
                  pl.BlockSpec((d, d), lambda i: (0, 0)), pl.BlockSpec((pd, d), lambda i: (0, 0))],
        out_specs=row,
        out_shape=jax.ShapeDtypeStruct((t, d), F32),
        compiler_params=_params("parallel"),
        name="ln2_ple",
    )(x1, y, p, g, b, wg, wp)


def kernel(x, p, w_in, b_in, gm_ln_g, gm_ln_b, gm_ws, gm_bs, w_gm_out, cv_w, cv_b, cv_ln_g, cv_ln_b,
           w_cv_out, w_o, ln1_g, ln1_b, peer_wq, peer_keys, peer_u, peer_v, ln2_g, ln2_b,
           ple_w_gate, ple_w_proj):
    bsz, seq, d = x.shape
    depth = w_in.shape[0]
    t = bsz * seq
    alpha = (2.0 * depth) ** 0.25
    groups, blk, _ = gm_ws.shape[1:]
    gdim = d // groups
    tm = min(512, seq)

    xf = x.reshape(t, d)
    for i in range(depth):
        row = lambda a: a[i].reshape(1, -1)
        bs_b = jnp.broadcast_to(gm_bs[i][:, :, None], (groups, blk, gdim))
        (sg, glu, ga, gb), (w_a, w_b, w_ob, wq_b, wg_b) = _inproj(
            xf, w_in[i].astype(BF16), row(b_in), row(gm_ln_g), row(gm_ln_b), gm_ws[i], bs_b,
            [w_gm_out[i], w_cv_out[i], w_o[i], peer_wq[i], ple_w_gate[i]], tm=min(1024, t), tn=gdim)
        m = _mix(sg, glu, cv_w[i], row(cv_b), row(cv_ln_g), row(cv_ln_b), w_a, w_b, ga, gb,
                 seq=seq, tm=tm, tn=1024)
        x1, x1b, x1t = _oproj(m, xf, w_ob, row(ln1_g), row(ln1_b), alpha=alpha, tm=tm)
        r2, b2, nb1, a1, ub, vt = _route(x1b, wq_b, peer_keys[i], peer_u[i], peer_v[i], tc=min(1024, t))
        y = _experts(x1t, ub, vt, r2, b2, nb1, a1, tm=tm, rows_per_step=8)
        xf = _tail(x1, y, p[i].reshape(t, -1), row(ln2_g), row(ln2_b),
                   wg_b, ple_w_proj[i].astype(BF16), alpha=alpha, tm=tm)
    return xf.reshape(bsz, seq, d)
```
